```python
import jax, jax.numpy as jnp
from jax import lax
import numpy as np

D_MODEL = 1024
BATCH = 32
SEQ = 2048
DEPTH = 2

N_A_LAYERS = DEPTH // 2
N_B_LAYERS = DEPTH - N_A_LAYERS
N_DENSE = (DEPTH + 1) // 2
N_MOE = DEPTH // 2
CONV_WIDTH = 31
N_HEADS = 8
HEAD_DIM = D_MODEL // N_HEADS
ROT_DIM = HEAD_DIM // 4
ROPE_THETA = 500000.0
MOBA_BLOCK = 256
MOBA_TOPK = 3
Q_CHUNK = 128
D_FF = ((8 * D_MODEL // 3 + 255) // 256) * 256
N_EXPERTS = 8
TOP_K = 2
D_FF_EXPERT = 7 * D_MODEL // 2
NORM_EPS = 1e-6
POS_OFFSET_MAX = 1024

kernel_name = 'hybrid_conformer_moba_yoco'


def _rms_norm(x, g):
    xf = x.astype(jnp.float32)
    y = xf * lax.rsqrt(jnp.mean(xf * xf, axis=-1, keepdims=True) + NORM_EPS)
    return (y * g.astype(jnp.float32)).astype(x.dtype)


def _layer_norm(x, g, b):
    xf = x.astype(jnp.float32)
    mu = jnp.mean(xf, axis=-1, keepdims=True)
    var = jnp.mean(jnp.square(xf - mu), axis=-1, keepdims=True)
    y = (xf - mu) * lax.rsqrt(var + NORM_EPS)
    return (y * g.astype(jnp.float32) + b.astype(jnp.float32)).astype(x.dtype)


def _ada(c, w, b, n):
    mod = jax.nn.silu(c) @ w + b
    return jnp.split(mod, n, axis=-1)


def _modulate(h, shift, scale):
    return h * (1.0 + scale[:, None, :]) + shift[:, None, :]


def _rotary(x, positions):
    half = ROT_DIM // 2
    inv_freq = ROPE_THETA ** (-jnp.arange(0, ROT_DIM, 2, dtype=jnp.float32) / ROT_DIM)
    ang = positions.astype(jnp.float32)[..., None] * inv_freq
    cos = jnp.cos(ang)[:, :, None, :]
    sin = jnp.sin(ang)[:, :, None, :]
    xr = x[..., :ROT_DIM].astype(jnp.float32)
    x1, x2 = xr[..., :half], xr[..., half:]
    rot = jnp.concatenate([x1 * cos - x2 * sin, x2 * cos + x1 * sin], axis=-1).astype(x.dtype)
    return jnp.concatenate([rot, x[..., ROT_DIM:]], axis=-1)


def _swiglu(h, w13, w2):
    a, g = jnp.split(h @ w13, 2, axis=-1)
    return (jax.nn.silu(a) * g) @ w2


def _conformer_conv(h, w1, b1, dw_w, dw_b, ln_g, ln_b, w2, b2):
    a, g = jnp.split(h @ w1 + b1, 2, axis=-1)
    u = a * jax.nn.sigmoid(g)
    z = lax.conv_general_dilated(u, dw_w[:, None, :], window_strides=(1,),
                                 padding=[(CONV_WIDTH - 1, 0)],
                                 dimension_numbers=('NWC', 'WIO', 'NWC'),
                                 feature_group_count=u.shape[-1]) + dw_b
    z = jax.nn.silu(_layer_norm(z, ln_g, ln_b))
    return z @ w2 + b2


def _shared_kv(x, c, positions, kv_ada_w, kv_ada_b, kv_norm_g, w_kv):
    b, s, _ = x.shape
    shift, scale = _ada(c, kv_ada_w, kv_ada_b, 2)
    h = _modulate(_rms_norm(x, kv_norm_g), shift, scale)
    k, v = jnp.split(h @ w_kv, 2, axis=-1)
    k = _rotary(k.reshape(b, s, N_HEADS, HEAD_DIM), positions)
    v = v.reshape(b, s, N_HEADS, HEAD_DIM)
    nb = -(-s // MOBA_BLOCK)
    pad = nb * MOBA_BLOCK - s
    k = jnp.pad(k, ((0, 0), (0, pad), (0, 0), (0, 0)))
    v = jnp.pad(v, ((0, 0), (0, pad), (0, 0), (0, 0)))
    kb = k.transpose(0, 2, 1, 3).reshape(b, N_HEADS, nb, MOBA_BLOCK, HEAD_DIM)
    vb = v.transpose(0, 2, 1, 3).reshape(b, N_HEADS, nb, MOBA_BLOCK, HEAD_DIM)
    kmean = jnp.mean(kb.astype(jnp.float32), axis=3).astype(kb.dtype)
    return kb, vb, kmean


def _moba_one_sequence(q, kb, vb, kmean):
    n_heads, seq, hd = q.shape
    nb = kb.shape[1]
    topk = min(MOBA_TOPK, max(nb - 1, 1))
    scale = hd ** -0.5
    head_idx = jnp.arange(n_heads)[:, None, None]
    blk_ids = jnp.arange(nb)

    def chunk(ci):
        t0 = ci * Q_CHUNK
        own = t0 // MOBA_BLOCK
        qc = lax.dynamic_slice_in_dim(q, t0, Q_CHUNK, axis=1)
        gate = jnp.einsum('hqd,hnd->hqn', qc, kmean).astype(jnp.float32)
        gate = jnp.where(blk_ids < own, gate, -jnp.inf)
        _, sel = lax.top_k(gate, topk)
        valid = sel < own
        k_sel = kb[head_idx, sel]
        v_sel = vb[head_idx, sel]
        s_sel = jnp.einsum('hqd,hqnbd->hqnb', qc, k_sel).astype(jnp.float32) * scale
        s_sel = jnp.where(valid[..., None], s_sel, -jnp.inf)
        k_own = lax.dynamic_index_in_dim(kb, own, axis=1, keepdims=False)
        v_own = lax.dynamic_index_in_dim(vb, own, axis=1, keepdims=False)
        s_own = jnp.einsum('hqd,hbd->hqb', qc, k_own).astype(jnp.float32) * scale
        q_pos = t0 + jnp.arange(Q_CHUNK)
        k_pos = own * MOBA_BLOCK + jnp.arange(MOBA_BLOCK)
        s_own = jnp.where(k_pos[None, None, :] <= q_pos[None, :, None], s_own, -jnp.inf)
        s = jnp.concatenate([s_sel.reshape(n_heads, Q_CHUNK, topk * MOBA_BLOCK), s_own], axis=-1)
        p = jax.nn.softmax(s, axis=-1).astype(vb.dtype)
        p_sel = p[..., :topk * MOBA_BLOCK].reshape(n_heads, Q_CHUNK, topk, MOBA_BLOCK)
        p_own = p[..., topk * MOBA_BLOCK:]
        return (jnp.einsum('hqnb,hqnbd->hqd', p_sel, v_sel)
                + jnp.einsum('hqb,hbd->hqd', p_own, v_own))

    out = lax.map(chunk, jnp.arange(seq // Q_CHUNK))
    return out.transpose(1, 0, 2, 3).reshape(n_heads, seq, hd)


def _moba_attention(q, kb, vb, kmean):
    return lax.map(lambda args: _moba_one_sequence(*args), (q, kb, vb, kmean))


def _moe_ffn(h, router_w, router_b, w13, w2):
    b, s, d = h.shape
    t = h.reshape(-1, d)
    logits = t.astype(jnp.float32) @ router_w.astype(jnp.float32) + router_b.astype(jnp.float32)
    top_val, top_idx = lax.top_k(logits, TOP_K)
    top_w = jax.nn.softmax(top_val, axis=-1)
    gates = jnp.sum(jax.nn.one_hot(top_idx, N_EXPERTS, dtype=jnp.float32) * top_w[..., None], axis=1)
    y = jnp.zeros(t.shape, jnp.float32)
    for e in range(N_EXPERTS):
        y = y + gates[:, e:e + 1] * _swiglu(t, w13[e], w2[e]).astype(jnp.float32)
    return y.astype(h.dtype).reshape(b, s, d)


def setup_inputs(seed: int = 0) -> dict:
    key = jax.random.key(seed)
    ks = iter(jax.random.split(key, 32))
    D = D_MODEL
    f32 = jnp.float32

    def nrm(shape, fan_in, mult=1.0):
        return jax.random.normal(next(ks), shape, f32) * (mult * fan_in ** -0.5)

    def small(shape, s=0.02):
        return jax.random.normal(next(ks), shape, f32) * s

    def gain(shape):
        return 1.0 + small(shape)

    x = jax.random.normal(next(ks), (BATCH, SEQ, D), f32)
    c = jax.random.normal(next(ks), (BATCH, D), f32)
    positions = (jnp.arange(SEQ, dtype=jnp.int32)[None, :]
                 + jax.random.randint(next(ks), (BATCH, 1), 0, POS_OFFSET_MAX, dtype=jnp.int32))
    return {
        'x': x, 'c': c, 'positions': positions,
        'ada_w': nrm((DEPTH, D, 6 * D), D, 0.5), 'ada_b': small((DEPTH, 6 * D)),
        'norm1_g': gain((DEPTH, D)), 'norm2_g': gain((DEPTH, D)),
        'conv_w1': nrm((N_A_LAYERS, D, 2 * D), D), 'conv_b1': small((N_A_LAYERS, 2 * D)),
        'conv_dw_w': nrm((N_A_LAYERS, CONV_WIDTH, D), CONV_WIDTH), 'conv_dw_b': small((N_A_LAYERS, D)),
        'conv_ln_g': gain((N_A_LAYERS, D)), 'conv_ln_b': small((N_A_LAYERS, D)),
        'conv_w2': nrm((N_A_LAYERS, D, D), D), 'conv_b2': small((N_A_LAYERS, D)),
        'kv_ada_w': nrm((D, 2 * D), D, 0.5), 'kv_ada_b': small((2 * D,)),
        'kv_norm_g': gain((D,)), 'w_kv': nrm((D, 2 * D), D),
        'w_q': nrm((N_B_LAYERS, D, D), D), 'w_o': nrm((N_B_LAYERS, D, D), D),
        'ffn_w13': nrm((N_DENSE, D, 2 * D_FF), D), 'ffn_w2': nrm((N_DENSE, D_FF, D), D_FF),
        'router_w': nrm((N_MOE, D, N_EXPERTS), D), 'router_b': small((N_MOE, N_EXPERTS), 0.01),
        'moe_w13': nrm((N_MOE, N_EXPERTS, D, 2 * D_FF_EXPERT), D),
        'moe_w2': nrm((N_MOE, N_EXPERTS, D_FF_EXPERT, D), D_FF_EXPERT),
        'final_g': gain((D,)),
    }


def reference(x, c, positions, ada_w, ada_b, norm1_g, norm2_g, conv_w1, conv_b1, conv_dw_w,
              conv_dw_b, conv_ln_g, conv_ln_b, conv_w2, conv_b2, kv_ada_w, kv_ada_b, kv_norm_g,
              w_kv, w_q, w_o, ffn_w13, ffn_w2, router_w, router_b, moe_w13, moe_w2, final_g):
    b, s, d = x.shape
    kb = vb = kmean = None
    for i in range(DEPTH):
        shift1, scale1, gate1, shift2, scale2, gate2 = _ada(c, ada_w[i], ada_b[i], 6)
        h = _modulate(_rms_norm(x, norm1_g[i]), shift1, scale1)
        if i < N_A_LAYERS:
            mix = _conformer_conv(h, conv_w1[i], conv_b1[i], conv_dw_w[i], conv_dw_b[i],
                                  conv_ln_g[i], conv_ln_b[i], conv_w2[i], conv_b2[i])
        else:
            j = i - N_A_LAYERS
            q = _rotary((h @ w_q[j]).reshape(b, s, N_HEADS, HEAD_DIM), positions)
            o = _moba_attention(q.transpose(0, 2, 1, 3), kb, vb, kmean)
            mix = o.transpose(0, 2, 1, 3).reshape(b, s, d) @ w_o[j]
        x = x + gate1[:, None, :] * mix
        h = _modulate(_rms_norm(x, norm2_g[i]), shift2, scale2)
        if i % 2 == 0:
            f = _swiglu(h, ffn_w13[i // 2], ffn_w2[i // 2])
        else:
            f = _moe_ffn(h, router_w[i // 2], router_b[i // 2], moe_w13[i // 2], moe_w2[i // 2])
        x = x + gate2[:, None, :] * f
        if i == N_A_LAYERS - 1:
            kb, vb, kmean = _shared_kv(x, c, positions, kv_ada_w, kv_ada_b, kv_norm_g, w_kv)
    return _rms_norm(x, final_g)
```

```python
import functools

import jax
import jax.numpy as jnp
from jax import lax
from jax.experimental import pallas as pl
from jax.experimental.pallas import tpu as pltpu

N_HEADS = 8
HEAD_DIM = 128
ROT_DIM = HEAD_DIM // 4
ROPE_THETA = 500000.0
MOBA_BLOCK = 256
MOBA_TOPK = 3
CONV_WIDTH = 31
N_EXPERTS = 8
TOP_K = 2
NORM_EPS = 1e-6

LANES = 128
SUBLANES = 8
CONV_HALO = 32
VMEM_LIMIT = 56 * 1024 * 1024

F32 = jnp.float32
BF16 = jnp.bfloat16
HIGHEST = lax.Precision.HIGHEST


def _cparams(sem):
    return pltpu.CompilerParams(dimension_semantics=sem, vmem_limit_bytes=VMEM_LIMIT)


def _rms_mod(x, g, shift, scale):
    y = x * lax.rsqrt(jnp.mean(x * x, axis=-1, keepdims=True) + NORM_EPS)
    return (y * g) * (1.0 + scale) + shift


def _silu(a):
    return a * jax.nn.sigmoid(a)


def _ada_kernel(c_ref, w_ref, b_ref, o_ref):
    o_ref[...] = jnp.dot(_silu(c_ref[...]), w_ref[...], preferred_element_type=F32,
                         precision=HIGHEST) + b_ref[...]


def _ada(c, w, b):
    bsz, d = c.shape
    n = w.shape[1]
    tn = min(n, 2048)
    return pl.pallas_call(
        _ada_kernel,
        grid=(n // tn,),
        in_specs=[pl.BlockSpec((bsz, d), lambda j: (0, 0)),
                  pl.BlockSpec((d, tn), lambda j: (0, j)),
                  pl.BlockSpec((1, tn), lambda j: (0, j))],
        out_specs=pl.BlockSpec((bsz, tn), lambda j: (0, j)),
        out_shape=jax.ShapeDtypeStruct((bsz, n), F32),
        compiler_params=_cparams(("arbitrary",)),
        name="ada",
    )(c, w, b.reshape(1, n))


CONV_ROWS = 64


def _conv_mixer_kernel(x_ref, mod_ref, g_ref, w1_ref, b1_ref, dw_ref, dwb_ref, lng_ref, lnb_ref,
                       w2_ref, b2_ref, o_ref, ubuf, zbuf):
    tm, d = x_ref.shape[1], x_ref.shape[2]
    j = pl.program_id(1)

    @pl.when(j == 0)
    def _():
        ubuf[0:CONV_HALO, :] = jnp.zeros((CONV_HALO, d), F32)

    @pl.when(j > 0)
    def _():
        ubuf[0:CONV_HALO, :] = ubuf[tm:tm + CONV_HALO, :]

    x = x_ref[0]
    mod = mod_ref[0]
    h = _rms_mod(x, g_ref[...], mod[0:1], mod[1:2]).astype(BF16)
    ag = jnp.dot(h, w1_ref[...], preferred_element_type=F32) + b1_ref[...]
    ubuf[CONV_HALO:CONV_HALO + tm, :] = ag[:, :d] * jax.nn.sigmoid(ag[:, d:])

    lead = CONV_HALO - (CONV_WIDTH - 1)
    win_rows = CONV_ROWS + CONV_HALO

    def chunk(r, carry):
        base = pl.multiple_of(r * CONV_ROWS, CONV_ROWS)
        for lc in range(d // LANES):
            lanes = slice(lc * LANES, (lc + 1) * LANES)
            win = ubuf[pl.ds(base, win_rows), lanes]
            acc = jnp.broadcast_to(dwb_ref[:, lanes], (CONV_ROWS, LANES))
            for phase in range(SUBLANES):
                taps = [k for k in range(CONV_WIDTH) if (lead + k) % SUBLANES == phase]
                if not taps:
                    continue
                span = (lead + taps[-1]) - phase + CONV_ROWS
                shifted = win[phase:phase + span, :]
                for k in taps:
                    off = lead + k - phase
                    acc = acc + dw_ref[k:k + 1, lanes] * shifted[off:off + CONV_ROWS, :]
            zbuf[pl.ds(base, CONV_ROWS), lanes] = acc
        return carry

    lax.fori_loop(0, tm // CONV_ROWS, chunk, 0)

    z = zbuf[...]
    mu = jnp.mean(z, axis=-1, keepdims=True)
    zc = z - mu
    var = jnp.mean(zc * zc, axis=-1, keepdims=True)
    zn = (zc * lax.rsqrt(var + NORM_EPS)) * lng_ref[...] + lnb_ref[...]
    mix = jnp.dot(_silu(zn).astype(BF16), w2_ref[...], preferred_element_type=F32) + b2_ref[...]
    o_ref[0] = x + mod[2:3] * mix


def _conv_mixer(x, mod, g, w1, b1, dw, dwb, lng, lnb, w2, b2, tm):
    bsz, s, d = x.shape
    row = lambda b, j: (0, 0)
    return pl.pallas_call(
        _conv_mixer_kernel,
        grid=(bsz, s // tm),
        in_specs=[pl.BlockSpec((1, tm, d), lambda b, j: (b, j, 0)),
                  pl.BlockSpec((1, 6, d), lambda b, j: (b, 0, 0)),
                  pl.BlockSpec((1, d), row),
                  pl.BlockSpec((d, 2 * d), row),
                  pl.BlockSpec((1, 2 * d), row),
                  pl.BlockSpec((CONV_WIDTH, d), row),
                  pl.BlockSpec((1, d), row),
                  pl.BlockSpec((1, d), row),
                  pl.BlockSpec((1, d), row),
                  pl.BlockSpec((d, d), row),
                  pl.BlockSpec((1, d), row)],
        out_specs=pl.BlockSpec((1, tm, d), lambda b, j: (b, j, 0)),
        out_shape=jax.ShapeDtypeStruct((bsz, s, d), F32),
        scratch_shapes=[pltpu.VMEM((tm + CONV_HALO, d), F32), pltpu.VMEM((tm, d), F32)],
        compiler_params=_cparams(("arbitrary", "arbitrary")),
        name="conv_mixer",
    )(x, mod, g.reshape(1, d), w1, b1.reshape(1, 2 * d), dw, dwb.reshape(1, d), lng.reshape(1, d),
      lnb.reshape(1, d), w2, b2.reshape(1, d))


def _ffn_kernel(x_ref, mod_ref, g_ref, wa_ref, wg_ref, w2_ref, o_ref, h_scr, acc_scr):
    j = pl.program_id(2)

    @pl.when(j == 0)
    def _():
        mod = mod_ref[0]
        h_scr[...] = _rms_mod(x_ref[0], g_ref[...], mod[3:4], mod[4:5]).astype(BF16)
        acc_scr[...] = jnp.zeros_like(acc_scr)

    h = h_scr[...]
    a = jnp.dot(h, wa_ref[...], preferred_element_type=F32)
    g = jnp.dot(h, wg_ref[...], preferred_element_type=F32)
    acc_scr[...] += jnp.dot((_silu(a) * g).astype(BF16), w2_ref[...], preferred_element_type=F32)

    @pl.when(j == pl.num_programs(2) - 1)
    def _():
        o_ref[0] = x_ref[0] + mod_ref[0][5:6] * acc_scr[...]


def _dense_ffn(x, mod, g, w13, w2, tm, tf):
    bsz, s, d = x.shape
    f = w2.shape[0]
    nf = f // tf
    return pl.pallas_call(
        _ffn_kernel,
        grid=(bsz, s // tm, nf),
        in_specs=[pl.BlockSpec((1, tm, d), lambda b, i, j: (b, i, 0)),
                  pl.BlockSpec((1, 6, d), lambda b, i, j: (b, 0, 0)),
                  pl.BlockSpec((1, d), lambda b, i, j: (0, 0)),
                  pl.BlockSpec((d, tf), lambda b, i, j: (0, j)),
                  pl.BlockSpec((d, tf), lambda b, i, j: (0, nf + j)),
                  pl.BlockSpec((tf, d), lambda b, i, j: (j, 0))],
        out_specs=pl.BlockSpec((1, tm, d), lambda b, i, j: (b, i, 0)),
        out_shape=jax.ShapeDtypeStruct((bsz, s, d), F32),
        scratch_shapes=[pltpu.VMEM((tm, d), BF16), pltpu.VMEM((tm, d), F32)],
        compiler_params=_cparams(("arbitrary", "arbitrary", "arbitrary")),
        name="dense_ffn",
    )(x, mod, g.reshape(1, d), w13, w13, w2)


def _rotary_heads(y, cos, sin_signed, low_half):
    outs = []
    for hh in range(y.shape[1] // HEAD_DIM):
        ys = y[:, hh * HEAD_DIM:(hh + 1) * HEAD_DIM]
        partner = jnp.where(low_half, pltpu.roll(ys, HEAD_DIM - ROT_DIM // 2, 1),
                            pltpu.roll(ys, ROT_DIM // 2, 1))
        outs.append(ys * cos + partner * sin_signed)
    return outs


def _kvq_kernel(x_ref, modkv_ref, mod_ref, gkv_ref, gq_ref, wkv_ref, wq_ref, pos_ref, freq_ref,
                q_ref, k_ref, v_ref, km_ref):
    tm, d = x_ref.shape[1], x_ref.shape[2]
    x = x_ref[0]
    inv = lax.rsqrt(jnp.mean(x * x, axis=-1, keepdims=True) + NORM_EPS)
    xn = x * inv
    modkv = modkv_ref[0]
    mod = mod_ref[0]
    hkv = ((xn * gkv_ref[...]) * (1.0 + modkv[1:2]) + modkv[0:1]).astype(BF16)
    hq = ((xn * gq_ref[...]) * (1.0 + mod[1:2]) + mod[0:1]).astype(BF16)
    kv = jnp.dot(hkv, wkv_ref[...], preferred_element_type=F32)
    qf = jnp.dot(hq, wq_ref[...], preferred_element_type=F32)

    ang = pos_ref[0].astype(F32) * freq_ref[...]
    lane = lax.broadcasted_iota(jnp.int32, (tm, HEAD_DIM), 1)
    low_half = lane < ROT_DIM // 2
    cos = jnp.cos(ang)
    sin = jnp.sin(ang)
    sin_signed = jnp.where(low_half, -sin, sin)

    v_ref[0] = kv[:, d:].astype(BF16)
    k_heads = _rotary_heads(kv[:, :d], cos, sin_signed, low_half)
    q_heads = _rotary_heads(qf, cos, sin_signed, low_half)
    for hh in range(N_HEADS):
        hs = slice(hh * HEAD_DIM, (hh + 1) * HEAD_DIM)
        q_ref[0, :, hs] = q_heads[hh].astype(BF16)
        k_ref[0, :, hs] = k_heads[hh].astype(BF16)
        for r in range(tm // MOBA_BLOCK):
            blk = k_heads[hh][r * MOBA_BLOCK:(r + 1) * MOBA_BLOCK, :]
            km_ref[0, r, :, hs] = jnp.mean(blk, axis=0, keepdims=True)


def _kvq(x, modkv, mod, gkv, gq, wkv, wq, pos, freq, tm):
    bsz, s, d = x.shape
    nb = s // MOBA_BLOCK
    row = lambda b, j: (0, 0)
    act = jax.ShapeDtypeStruct((bsz, s, d), BF16)
    return pl.pallas_call(
        _kvq_kernel,
        grid=(bsz, s // tm),
        in_specs=[pl.BlockSpec((1, tm, d), lambda b, j: (b, j, 0)),
                  pl.BlockSpec((1, 2, d), lambda b, j: (b, 0, 0)),
                  pl.BlockSpec((1, 6, d), lambda b, j: (b, 0, 0)),
                  pl.BlockSpec((1, d), row),
                  pl.BlockSpec((1, d), row),
                  pl.BlockSpec((d, 2 * d), row),
                  pl.BlockSpec((d, d), row),
                  pl.BlockSpec((1, tm, 1), lambda b, j: (b, j, 0)),
                  pl.BlockSpec((1, HEAD_DIM), row)],
        out_specs=[pl.BlockSpec((1, tm, d), lambda b, j: (b, j, 0)),
                   pl.BlockSpec((1, tm, d), lambda b, j: (b, j, 0)),
                   pl.BlockSpec((1, tm, d), lambda b, j: (b, j, 0)),
                   pl.BlockSpec((1, tm // MOBA_BLOCK, 1, d), lambda b, j: (b, j, 0, 0))],
        out_shape=[act, act, act, jax.ShapeDtypeStruct((bsz, nb, 1, d), F32)],
        compiler_params=_cparams(("arbitrary", "arbitrary")),
        name="kvq_proj",
    )(x, modkv, mod, gkv.reshape(1, d), gq.reshape(1, d), wkv, wq, pos, freq)


def _attn_kernel(q_ref, k_ref, v_ref, km_ref, o_ref, m_scr, l_scr, acc_scr, *, topk):
    i = pl.program_id(2)
    tq = q_ref.shape[1]
    nb = km_ref.shape[2]
    scale = HEAD_DIM ** -0.5
    q = q_ref[0]
    contract_last = (((1,), (1,)), ((), ()))

    gate = lax.dot_general(q.astype(F32), km_ref[0, 0], contract_last, precision=HIGHEST,
                           preferred_element_type=F32)
    blk = lax.broadcasted_iota(jnp.int32, (tq, nb), 1)
    past = blk < i
    gate = jnp.where(past, gate, -jnp.inf)
    rank = jnp.zeros((tq, nb), jnp.int32)
    for m in range(nb):
        gm = gate[:, m:m + 1]
        beats = jnp.where(gm > gate, 1, jnp.where((gm == gate) & (blk > m), 1, 0))
        rank = rank + beats
    sel = jnp.where(past & (rank < topk), 1.0, 0.0)

    k_own = k_ref[0, pl.ds(pl.multiple_of(i * MOBA_BLOCK, MOBA_BLOCK), MOBA_BLOCK), :]
    v_own = v_ref[0, pl.ds(pl.multiple_of(i * MOBA_BLOCK, MOBA_BLOCK), MOBA_BLOCK), :]
    s = lax.dot_general(q, k_own, contract_last, preferred_element_type=F32) * scale
    rows = lax.broadcasted_iota(jnp.int32, s.shape, 0)
    cols = lax.broadcasted_iota(jnp.int32, s.shape, 1)
    s = jnp.where(cols <= rows, s, -jnp.inf)
    m0 = jnp.max(s, axis=-1, keepdims=True)
    p = jnp.exp(s - m0)
    m_scr[...] = m0
    l_scr[...] = jnp.sum(p, axis=-1, keepdims=True)
    acc_scr[...] = jnp.dot(p.astype(BF16), v_own, preferred_element_type=F32)

    def past_block(n, carry):
        start = pl.multiple_of(n * MOBA_BLOCK, MOBA_BLOCK)
        kn = k_ref[0, pl.ds(start, MOBA_BLOCK), :]
        vn = v_ref[0, pl.ds(start, MOBA_BLOCK), :]
        sn = lax.dot_general(q, kn, contract_last, preferred_element_type=F32) * scale
        chosen = jnp.sum(jnp.where(blk == n, sel, 0.0), axis=-1, keepdims=True) > 0.0
        sn = jnp.where(chosen, sn, -jnp.inf)
        m_old = m_scr[...]
        m_new = jnp.maximum(m_old, jnp.max(sn, axis=-1, keepdims=True))
        alpha = jnp.exp(m_old - m_new)
        pn = jnp.exp(sn - m_new)
        l_scr[...] = alpha * l_scr[...] + jnp.sum(pn, axis=-1, keepdims=True)
        acc_scr[...] = alpha * acc_scr[...] + jnp.dot(pn.astype(BF16), vn, preferred_element_type=F32)
        m_scr[...] = m_new
        return carry

    lax.fori_loop(0, i, past_block, 0)
    o_ref[0] = (acc_scr[...] / l_scr[...]).astype(o_ref.dtype)


def _attention(q, k, v, kmean, topk):
    bsz, s, d = q.shape
    nb = s // MOBA_BLOCK
    tq = MOBA_BLOCK
    return pl.pallas_call(
        functools.partial(_attn_kernel, topk=topk),
        grid=(bsz, N_HEADS, s // tq),
        in_specs=[pl.BlockSpec((1, tq, HEAD_DIM), lambda b, h, i: (b, i, h)),
                  pl.BlockSpec((1, s, HEAD_DIM), lambda b, h, i: (b, 0, h)),
                  pl.BlockSpec((1, s, HEAD_DIM), lambda b, h, i: (b, 0, h)),
                  pl.BlockSpec((1, 1, nb, HEAD_DIM), lambda b, h, i: (b, h, 0, 0))],
        out_specs=pl.BlockSpec((1, tq, HEAD_DIM), lambda b, h, i: (b, i, h)),
        out_shape=jax.ShapeDtypeStruct((bsz, s, d), BF16),
        scratch_shapes=[pltpu.VMEM((tq, 1), F32), pltpu.VMEM((tq, 1), F32),
                        pltpu.VMEM((tq, HEAD_DIM), F32)],
        compiler_params=_cparams(("arbitrary", "arbitrary", "arbitrary")),
        name="moba_attention",
    )(q, k, v, kmean)


def _oproj_router_kernel(o_ref, x_ref, mod_ref, g_ref, wo_ref, rw_ref, rb_ref, x_out, h_out, route_out):
    mod = mod_ref[0]
    mix = jnp.dot(o_ref[0], wo_ref[...], preferred_element_type=F32)
    x = x_ref[0] + mod[2:3] * mix
    x_out[0] = x
    h = _rms_mod(x, g_ref[...], mod[3:4], mod[4:5])
    h_out[0] = h

    logits = jnp.dot(h, rw_ref[...], preferred_element_type=F32, precision=HIGHEST) + rb_ref[...]
    lane = lax.broadcasted_iota(jnp.int32, logits.shape, 1)
    logits = jnp.where(lane < N_EXPERTS, logits, -jnp.inf)
    v1 = jnp.max(logits, axis=-1, keepdims=True)
    i1 = jnp.min(jnp.where(logits == v1, lane, LANES), axis=-1, keepdims=True)
    rest = jnp.where(lane == i1, -jnp.inf, logits)
    v2 = jnp.max(rest, axis=-1, keepdims=True)
    i2 = jnp.min(jnp.where(rest == v2, lane, LANES), axis=-1, keepdims=True)
    e2 = jnp.exp(v2 - v1)
    den = 1.0 + e2
    route = jnp.where(lane == 0, i1.astype(F32),
                      jnp.where(lane == 1, i2.astype(F32),
                                jnp.where(lane == 2, 1.0 / den,
                                          jnp.where(lane == 3, e2 / den, 0.0))))
    route_out[0] = route


def _oproj_router(o, x, mod, g, wo, rw, rb, tm):
    bsz, s, d = x.shape
    row = lambda b, j: (0, 0)
    tile = lambda b, j: (b, j, 0)
    return pl.pallas_call(
        _oproj_router_kernel,
        grid=(bsz, s // tm),
        in_specs=[pl.BlockSpec((1, tm, d), tile),
                  pl.BlockSpec((1, tm, d), tile),
                  pl.BlockSpec((1, 6, d), lambda b, j: (b, 0, 0)),
                  pl.BlockSpec((1, d), row),
                  pl.BlockSpec((d, d), row),
                  pl.BlockSpec((d, LANES), row),
                  pl.BlockSpec((1, LANES), row)],
        out_specs=[pl.BlockSpec((1, tm, d), tile),
                   pl.BlockSpec((1, tm, d), tile),
                   pl.BlockSpec((1, tm, LANES), tile)],
        out_shape=[jax.ShapeDtypeStruct((bsz, s, d), F32),
                   jax.ShapeDtypeStruct((bsz, s, d), F32),
                   jax.ShapeDtypeStruct((bsz, s, LANES), F32)],
        compiler_params=_cparams(("arbitrary", "arbitrary")),
        name="oproj_router",
    )(o, x, mod, g.reshape(1, d), wo, rw, rb)


def _dispatch_copy(slot_ref, h_hbm, hs_hbm, sem, base, r):
    tok = lax.shift_right_logical(base + r, 1)
    return pltpu.make_async_copy(h_hbm.at[pl.ds(tok, 1)], hs_hbm.at[pl.ds(slot_ref[0, 0, r], 1)], sem)


def _dispatch_kernel(slot_ref, h_hbm, hs_in, hs_out, sem):
    del hs_in
    chunk = slot_ref.shape[2]
    base = pl.program_id(0) * chunk

    def start(r, c):
        _dispatch_copy(slot_ref, h_hbm, hs_out, sem, base, r).start()
        return c

    def wait(r, c):
        _dispatch_copy(slot_ref, h_hbm, hs_out, sem, base, r).wait()
        return c

    lax.fori_loop(0, chunk, start, 0)
    lax.fori_loop(0, chunk, wait, 0)


def _dispatch(slots, h, n_slots, chunk):
    t, d = h.shape
    n = slots.shape[0]
    hs0 = jnp.zeros((n_slots, d), h.dtype)
    return pl.pallas_call(
        _dispatch_kernel,
        grid=(n // chunk,),
        in_specs=[pl.BlockSpec((1, 1, chunk), lambda i: (i, 0, 0), memory_space=pltpu.SMEM),
                  pl.BlockSpec(memory_space=pl.ANY),
                  pl.BlockSpec(memory_space=pl.ANY)],
        out_specs=pl.BlockSpec(memory_space=pl.ANY),
        out_shape=jax.ShapeDtypeStruct((n_slots, d), h.dtype),
        scratch_shapes=[pltpu.SemaphoreType.DMA(())],
        input_output_aliases={2: 0},
        compiler_params=_cparams(("arbitrary",)),
        name="moe_dispatch",
    )(slots.reshape(n // chunk, 1, chunk), h, hs0)


def _moe_kernel(te_ref, nv_ref, hs_ref, wa_ref, wg_ref, w2_ref, o_ref, h_scr, acc_scr):
    del te_ref
    i = pl.program_id(0)
    j = pl.program_id(1)
    valid = i < nv_ref[0]

    @pl.when(valid & (j == 0))
    def _():
        h_scr[...] = hs_ref[...].astype(BF16)
        acc_scr[...] = jnp.zeros_like(acc_scr)

    @pl.when(valid)
    def _():
        h = h_scr[...]
        a = jnp.dot(h, wa_ref[0], preferred_element_type=F32)
        g = jnp.dot(h, wg_ref[0], preferred_element_type=F32)
        acc_scr[...] += jnp.dot((_silu(a) * g).astype(BF16), w2_ref[0], preferred_element_type=F32)

    last = j == pl.num_programs(1) - 1

    @pl.when(valid & last)
    def _():
        o_ref[...] = acc_scr[...]

    @pl.when(jnp.logical_not(valid) & last)
    def _():
        o_ref[...] = jnp.zeros_like(o_ref)


def _moe_ffn(tile_expert, n_valid, hs, w13, w2, tm, tf):
    n_slots, d = hs.shape
    f = w2.shape[1]
    nf = f // tf
    nt = n_slots // tm

    def row_idx(i, j, te, nv):
        return (jnp.minimum(i, nv[0] - 1), 0)

    def ff(i, j, nv):
        return jnp.where(i < nv[0], j, nf - 1)

    grid_spec = pltpu.PrefetchScalarGridSpec(
        num_scalar_prefetch=2,
        grid=(nt, nf),
        in_specs=[pl.BlockSpec((tm, d), row_idx),
                  pl.BlockSpec((1, d, tf), lambda i, j, te, nv: (te[i], 0, ff(i, j, nv))),
                  pl.BlockSpec((1, d, tf), lambda i, j, te, nv: (te[i], 0, nf + ff(i, j, nv))),
                  pl.BlockSpec((1, tf, d), lambda i, j, te, nv: (te[i], ff(i, j, nv), 0))],
        out_specs=pl.BlockSpec((tm, d), lambda i, j, te, nv: (i, 0)),
        scratch_shapes=[pltpu.VMEM((tm, d), BF16), pltpu.VMEM((tm, d), F32)],
    )
    return pl.pallas_call(
        _moe_kernel,
        grid_spec=grid_spec,
        out_shape=jax.ShapeDtypeStruct((n_slots, d), F32),
        compiler_params=_cparams(("arbitrary", "arbitrary")),
        name="moe_ffn",
    )(tile_expert, n_valid, hs, w13, w13, w2)


def _combine_copy(slot_ref, ys_hbm, buf, sem, r, k):
    return pltpu.make_async_copy(ys_hbm.at[pl.ds(slot_ref[0, 0, 2 * r + k], 1)],
                                 buf.at[k, pl.ds(r, 1)], sem)


def _combine_kernel(slot_ref, x_ref, route_ref, mod_ref, g_ref, ys_hbm, o_ref, buf, sem):
    tc = x_ref.shape[1]

    def start(r, c):
        for k in range(TOP_K):
            _combine_copy(slot_ref, ys_hbm, buf, sem, r, k).start()
        return c

    def wait(r, c):
        for k in range(TOP_K):
            _combine_copy(slot_ref, ys_hbm, buf, sem, r, k).wait()
        return c

    lax.fori_loop(0, tc, start, 0)
    lax.fori_loop(0, tc, wait, 0)

    route = route_ref[0]
    y = route[:, 2:3] * buf[0] + route[:, 3:4] * buf[1]
    x = x_ref[0] + mod_ref[0][5:6] * y
    o_ref[0] = (x * lax.rsqrt(jnp.mean(x * x, axis=-1, keepdims=True) + NORM_EPS)) * g_ref[...]


def _combine(slots, x, route, mod, g, ys, tc):
    bsz, s, d = x.shape
    nt = s // tc
    tile = lambda b, j: (b, j, 0)
    return pl.pallas_call(
        _combine_kernel,
        grid=(bsz, nt),
        in_specs=[pl.BlockSpec((1, 1, TOP_K * tc), lambda b, j: (b * nt + j, 0, 0), memory_space=pltpu.SMEM),
                  pl.BlockSpec((1, tc, d), tile),
                  pl.BlockSpec((1, tc, LANES), tile),
                  pl.BlockSpec((1, 6, d), lambda b, j: (b, 0, 0)),
                  pl.BlockSpec((1, d), lambda b, j: (0, 0)),
                  pl.BlockSpec(memory_space=pl.ANY)],
        out_specs=pl.BlockSpec((1, tc, d), tile),
        out_shape=jax.ShapeDtypeStruct((bsz, s, d), F32),
        scratch_shapes=[pltpu.VMEM((TOP_K, tc, d), F32), pltpu.SemaphoreType.DMA(())],
        compiler_params=_cparams(("arbitrary", "arbitrary")),
        name="moe_combine",
    )(slots.reshape(bsz * nt, 1, TOP_K * tc), x, route, mod, g.reshape(1, d), ys)


def _routing_plan(route, tm):
    t = route.shape[0]
    flat_e = route[:, :TOP_K].astype(jnp.int32).reshape(-1)
    onehot = (flat_e[:, None] == jnp.arange(N_EXPERTS, dtype=jnp.int32)[None, :]).astype(jnp.int32)
    csum = jnp.cumsum(onehot, axis=0)
    rank = jnp.sum((csum - onehot) * onehot, axis=1)
    counts = csum[-1]
    padded = ((counts + tm - 1) // tm) * tm
    ends = jnp.cumsum(padded)
    starts = ends - padded
    slots = jnp.sum(starts[None, :] * onehot, axis=1) + rank
    n_tiles = TOP_K * t // tm + N_EXPERTS
    tile_start = jnp.arange(n_tiles, dtype=jnp.int32) * tm
    n_valid = (ends[-1] // tm).astype(jnp.int32)
    tile_expert = jnp.sum((tile_start[:, None] >= ends[None, :]).astype(jnp.int32), axis=1)
    last_expert = jnp.sum((ends[-1] - 1 >= ends).astype(jnp.int32))
    tile_expert = jnp.where(tile_start < ends[-1], tile_expert, last_expert).astype(jnp.int32)
    return slots.astype(jnp.int32), tile_expert, n_valid.reshape(1), n_tiles


def kernel(x, c, positions, ada_w, ada_b, norm1_g, norm2_g, conv_w1, conv_b1, conv_dw_w, conv_dw_b,
           conv_ln_g, conv_ln_b, conv_w2, conv_b2, kv_ada_w, kv_ada_b, kv_norm_g, w_kv, w_q, w_o,
           ffn_w13, ffn_w2, router_w, router_b, moe_w13, moe_w2, final_g):
    bsz, s, d = x.shape
    assert ada_w.shape[0] == 2 and d == N_HEADS * HEAD_DIM and s % MOBA_BLOCK == 0
    nb = s // MOBA_BLOCK
    topk = min(MOBA_TOPK, max(nb - 1, 1))
    tm = 512
    moe_tm = 1024

    mod0 = _ada(c, ada_w[0], ada_b[0]).reshape(bsz, 6, d)
    mod1 = _ada(c, ada_w[1], ada_b[1]).reshape(bsz, 6, d)
    modkv = _ada(c, kv_ada_w, kv_ada_b).reshape(bsz, 2, d)

    x = _conv_mixer(x, mod0, norm1_g[0], conv_w1[0].astype(BF16), conv_b1[0], conv_dw_w[0], conv_dw_b[0],
                    conv_ln_g[0], conv_ln_b[0], conv_w2[0].astype(BF16), conv_b2[0], tm)
    x = _dense_ffn(x, mod0, norm2_g[0], ffn_w13[0].astype(BF16), ffn_w2[0].astype(BF16), tm, 1408)

    inv_freq = ROPE_THETA ** (-jnp.arange(0, ROT_DIM, 2, dtype=F32) / ROT_DIM)
    freq = jnp.concatenate([inv_freq, inv_freq, jnp.zeros((HEAD_DIM - ROT_DIM,), F32)]).reshape(1, HEAD_DIM)
    q, k, v, kmean = _kvq(x, modkv, mod1, kv_norm_g, norm1_g[1], w_kv.astype(BF16), w_q[0].astype(BF16),
                          positions.reshape(bsz, s, 1), freq, tm)
    kmean = kmean.reshape(bsz, nb, N_HEADS, HEAD_DIM).transpose(0, 2, 1, 3)

    o = _attention(q, k, v, kmean, topk)
    rw = jnp.zeros((d, LANES), F32).at[:, :N_EXPERTS].set(router_w[0])
    rb = jnp.zeros((1, LANES), F32).at[0, :N_EXPERTS].set(router_b[0])
    x, h, route = _oproj_router(o, x, mod1, norm2_g[1], w_o[0].astype(BF16), rw, rb, tm)

    slots, tile_expert, n_valid, n_tiles = _routing_plan(route.reshape(bsz * s, LANES), moe_tm)
    hs = _dispatch(slots, h.reshape(bsz * s, d), n_tiles * moe_tm, 1024)
    ys = _moe_ffn(tile_expert, n_valid, hs, moe_w13[0].astype(BF16), moe_w2[0].astype(BF16), moe_tm, 512)
    return _combine(slots, x, route, mod1, final_g, ys, 256)
```

```python
import functools

import jax
import jax.numpy as jnp
from jax import lax
from jax.experimental import pallas as pl
from jax.experimental.pallas import tpu as pltpu

N_HEADS = 8
HEAD_DIM = 128
ROT_DIM = HEAD_DIM // 4
ROPE_THETA = 500000.0
MOBA_BLOCK = 256
MOBA_TOPK = 3
CONV_WIDTH = 31
N_EXPERTS = 8
TOP_K = 2
NORM_EPS = 1e-6
LOG2_E = 1.4426950408889634

LANES = 128
SUBLANES = 8
BF16_ROWS = 16
CONV_HALO = 32
VMEM_LIMIT = 56 * 1024 * 1024

F32 = jnp.float32
BF16 = jnp.bfloat16
HIGHEST = lax.Precision.HIGHEST


def _cparams(sem):
    return pltpu.CompilerParams(dimension_semantics=sem, vmem_limit_bytes=VMEM_LIMIT)


def _rms_mod(x, g, shift, scale):
    y = x * lax.rsqrt(jnp.mean(x * x, axis=-1, keepdims=True) + NORM_EPS)
    return (y * g) * (1.0 + scale) + shift


def _silu(a):
    return a * jax.nn.sigmoid(a)


def _ada_kernel(c_ref, w_ref, b_ref, o_ref):
    o_ref[...] = jnp.dot(_silu(c_ref[...]), w_ref[...], preferred_element_type=F32,
                         precision=HIGHEST) + b_ref[...]


def _ada(c, w, b):
    bsz, d = c.shape
    n = w.shape[1]
    tn = min(n, 2048)
    return pl.pallas_call(
        _ada_kernel,
        grid=(n // tn,),
        in_specs=[pl.BlockSpec((bsz, d), lambda j: (0, 0)),
                  pl.BlockSpec((d, tn), lambda j: (0, j)),
                  pl.BlockSpec((1, tn), lambda j: (0, j))],
        out_specs=pl.BlockSpec((bsz, tn), lambda j: (0, j)),
        out_shape=jax.ShapeDtypeStruct((bsz, n), F32),
        compiler_params=_cparams(("arbitrary",)),
        name="ada",
    )(c, w, b.reshape(1, n))


CONV_ROWS = 64


def _conv_mixer_kernel(x_ref, mod_ref, g_ref, w1_ref, b1_ref, dw_ref, dwb_ref, lng_ref, lnb_ref,
                       w2_ref, b2_ref, o_ref, ubuf, zbuf):
    tm, d = x_ref.shape[1], x_ref.shape[2]
    j = pl.program_id(1)

    @pl.when(j == 0)
    def _():
        ubuf[0:CONV_HALO, :] = jnp.zeros((CONV_HALO, d), F32)

    @pl.when(j > 0)
    def _():
        ubuf[0:CONV_HALO, :] = ubuf[tm:tm + CONV_HALO, :]

    x = x_ref[0]
    mod = mod_ref[0]
    h = _rms_mod(x, g_ref[...], mod[0:1], mod[1:2]).astype(BF16)
    ag = jnp.dot(h, w1_ref[...], preferred_element_type=F32) + b1_ref[...]
    ubuf[CONV_HALO:CONV_HALO + tm, :] = ag[:, :d] * jax.nn.sigmoid(ag[:, d:])

    lead = CONV_HALO - (CONV_WIDTH - 1)
    win_rows = CONV_ROWS + CONV_HALO

    def chunk(r, carry):
        base = pl.multiple_of(r * CONV_ROWS, CONV_ROWS)
        for lc in range(d // LANES):
            lanes = slice(lc * LANES, (lc + 1) * LANES)
            win = ubuf[pl.ds(base, win_rows), lanes]
            acc = jnp.broadcast_to(dwb_ref[:, lanes], (CONV_ROWS, LANES))
            for phase in range(SUBLANES):
                taps = [k for k in range(CONV_WIDTH) if (lead + k) % SUBLANES == phase]
                if not taps:
                    continue
                shifted = win if phase == 0 else pltpu.roll(win, win_rows - phase, 0)
                for k in taps:
                    off = lead + k - phase
                    acc = acc + dw_ref[k:k + 1, lanes] * shifted[off:off + CONV_ROWS, :]
            zbuf[pl.ds(base, CONV_ROWS), lanes] = acc
        return carry

    lax.fori_loop(0, tm // CONV_ROWS, chunk, 0)

    z = zbuf[...]
    mu = jnp.mean(z, axis=-1, keepdims=True)
    zc = z - mu
    var = jnp.mean(zc * zc, axis=-1, keepdims=True)
    zn = (zc * lax.rsqrt(var + NORM_EPS)) * lng_ref[...] + lnb_ref[...]
    mix = jnp.dot(_silu(zn).astype(BF16), w2_ref[...], preferred_element_type=F32) + b2_ref[...]
    o_ref[0] = x + mod[2:3] * mix


def _conv_mixer(x, mod, g, w1, b1, dw, dwb, lng, lnb, w2, b2, tm):
    bsz, s, d = x.shape
    row = lambda b, j: (0, 0)
    return pl.pallas_call(
        _conv_mixer_kernel,
        grid=(bsz, s // tm),
        in_specs=[pl.BlockSpec((1, tm, d), lambda b, j: (b, j, 0)),
                  pl.BlockSpec((1, 6, d), lambda b, j: (b, 0, 0)),
                  pl.BlockSpec((1, d), row),
                  pl.BlockSpec((d, 2 * d), row),
                  pl.BlockSpec((1, 2 * d), row),
                  pl.BlockSpec((CONV_WIDTH, d), row),
                  pl.BlockSpec((1, d), row),
                  pl.BlockSpec((1, d), row),
                  pl.BlockSpec((1, d), row),
                  pl.BlockSpec((d, d), row),
                  pl.BlockSpec((1, d), row)],
        out_specs=pl.BlockSpec((1, tm, d), lambda b, j: (b, j, 0)),
        out_shape=jax.ShapeDtypeStruct((bsz, s, d), F32),
        scratch_shapes=[pltpu.VMEM((tm + CONV_HALO, d), F32), pltpu.VMEM((tm, d), F32)],
        compiler_params=_cparams(("arbitrary", "arbitrary")),
        name="conv_mixer",
    )(x, mod, g.reshape(1, d), w1, b1.reshape(1, 2 * d), dw, dwb.reshape(1, d), lng.reshape(1, d),
      lnb.reshape(1, d), w2, b2.reshape(1, d))


def _ffn_kernel(x_ref, mod_ref, g_ref, wa_ref, wg_ref, w2_ref, o_ref, h_scr, acc_scr):
    j = pl.program_id(2)

    @pl.when(j == 0)
    def _():
        mod = mod_ref[0]
        h_scr[...] = _rms_mod(x_ref[0], g_ref[...], mod[3:4], mod[4:5]).astype(BF16)
        acc_scr[...] = jnp.zeros_like(acc_scr)

    h = h_scr[...]
    a = jnp.dot(h, wa_ref[...], preferred_element_type=F32)
    g = jnp.dot(h, wg_ref[...], preferred_element_type=F32)
    acc_scr[...] += jnp.dot((_silu(a) * g).astype(BF16), w2_ref[...], preferred_element_type=F32)

    @pl.when(j == pl.num_programs(2) - 1)
    def _():
        o_ref[0] = x_ref[0] + mod_ref[0][5:6] * acc_scr[...]


def _dense_ffn(x, mod, g, w13, w2, tm, tf):
    bsz, s, d = x.shape
    f = w2.shape[0]
    nf = f // tf
    return pl.pallas_call(
        _ffn_kernel,
        grid=(bsz, s // tm, nf),
        in_specs=[pl.BlockSpec((1, tm, d), lambda b, i, j: (b, i, 0)),
                  pl.BlockSpec((1, 6, d), lambda b, i, j: (b, 0, 0)),
                  pl.BlockSpec((1, d), lambda b, i, j: (0, 0)),
                  pl.BlockSpec((d, tf), lambda b, i, j: (0, j)),
                  pl.BlockSpec((d, tf), lambda b, i, j: (0, nf + j)),
                  pl.BlockSpec((tf, d), lambda b, i, j: (j, 0))],
        out_specs=pl.BlockSpec((1, tm, d), lambda b, i, j: (b, i, 0)),
        out_shape=jax.ShapeDtypeStruct((bsz, s, d), F32),
        scratch_shapes=[pltpu.VMEM((tm, d), BF16), pltpu.VMEM((tm, d), F32)],
        compiler_params=_cparams(("arbitrary", "arbitrary", "arbitrary")),
        name="dense_ffn",
    )(x, mod, g.reshape(1, d), w13, w13, w2)


def _rotary_heads(y, cos, sin_signed, low_half):
    outs = []
    for hh in range(y.shape[1] // HEAD_DIM):
        ys = y[:, hh * HEAD_DIM:(hh + 1) * HEAD_DIM]
        partner = jnp.where(low_half, pltpu.roll(ys, HEAD_DIM - ROT_DIM // 2, 1),
                            pltpu.roll(ys, ROT_DIM // 2, 1))
        outs.append(ys * cos + partner * sin_signed)
    return outs


def _kvq_kernel(x_ref, modkv_ref, mod_ref, gkv_ref, gq_ref, wkv_ref, wq_ref, pos_ref, freq_ref,
                q_ref, k_ref, vt_ref, km_ref):
    tm, d = x_ref.shape[1], x_ref.shape[2]
    x = x_ref[0]
    inv = lax.rsqrt(jnp.mean(x * x, axis=-1, keepdims=True) + NORM_EPS)
    xn = x * inv
    modkv = modkv_ref[0]
    mod = mod_ref[0]
    hkv = ((xn * gkv_ref[...]) * (1.0 + modkv[1:2]) + modkv[0:1]).astype(BF16)
    hq = ((xn * gq_ref[...]) * (1.0 + mod[1:2]) + mod[0:1]).astype(BF16)
    kv = jnp.dot(hkv, wkv_ref[...], preferred_element_type=F32)
    qf = jnp.dot(hq, wq_ref[...], preferred_element_type=F32)

    ang = pos_ref[0].astype(F32) * freq_ref[...]
    lane = lax.broadcasted_iota(jnp.int32, (tm, HEAD_DIM), 1)
    low_half = lane < ROT_DIM // 2
    cos = jnp.cos(ang)
    sin = jnp.sin(ang)
    sin_signed = jnp.where(low_half, -sin, sin)

    vt_ref[0] = kv[:, d:].T.astype(BF16)
    k_heads = _rotary_heads(kv[:, :d], cos, sin_signed, low_half)
    q_heads = _rotary_heads(qf, cos, sin_signed, low_half)
    for hh in range(N_HEADS):
        hs = slice(hh * HEAD_DIM, (hh + 1) * HEAD_DIM)
        q_ref[0, :, hs] = q_heads[hh].astype(BF16)
        k_ref[0, :, hs] = k_heads[hh].astype(BF16)
        for r in range(tm // MOBA_BLOCK):
            blk = k_heads[hh][r * MOBA_BLOCK:(r + 1) * MOBA_BLOCK, :]
            km_ref[0, r, :, hs] = jnp.mean(blk, axis=0, keepdims=True)


def _kvq(x, modkv, mod, gkv, gq, wkv, wq, pos, freq, tm):
    bsz, s, d = x.shape
    nb = s // MOBA_BLOCK
    row = lambda b, j: (0, 0)
    act = jax.ShapeDtypeStruct((bsz, s, d), BF16)
    return pl.pallas_call(
        _kvq_kernel,
        grid=(bsz, s // tm),
        in_specs=[pl.BlockSpec((1, tm, d), lambda b, j: (b, j, 0)),
                  pl.BlockSpec((1, 2, d), lambda b, j: (b, 0, 0)),
                  pl.BlockSpec((1, 6, d), lambda b, j: (b, 0, 0)),
                  pl.BlockSpec((1, d), row),
                  pl.BlockSpec((1, d), row),
                  pl.BlockSpec((d, 2 * d), row),
                  pl.BlockSpec((d, d), row),
                  pl.BlockSpec((1, tm, 1), lambda b, j: (b, j, 0)),
                  pl.BlockSpec((1, HEAD_DIM), row)],
        out_specs=[pl.BlockSpec((1, tm, d), lambda b, j: (b, j, 0)),
                   pl.BlockSpec((1, tm, d), lambda b, j: (b, j, 0)),
                   pl.BlockSpec((1, d, tm), lambda b, j: (b, 0, j)),
                   pl.BlockSpec((1, tm // MOBA_BLOCK, 1, d), lambda b, j: (b, j, 0, 0))],
        out_shape=[act, act, jax.ShapeDtypeStruct((bsz, d, s), BF16),
                   jax.ShapeDtypeStruct((bsz, nb, 1, d), F32)],
        compiler_params=_cparams(("arbitrary", "arbitrary")),
        name="kvq_proj",
    )(x, modkv, mod, gkv.reshape(1, d), gq.reshape(1, d), wkv, wq, pos, freq)


def _attn_tile(own, nb, q, km, k_ref, vt_ref, o_ref, s_scr, p_scr, topk):
    contract_last = (((1,), (1,)), ((), ()))
    tq = q.shape[0]

    k_own = k_ref[0, own * MOBA_BLOCK:(own + 1) * MOBA_BLOCK, :]
    ext = lax.dot_general(jnp.concatenate([k_own, km], axis=0), q, contract_last,
                          preferred_element_type=F32)
    s = ext[:MOBA_BLOCK]
    causal = lax.broadcasted_iota(jnp.int32, s.shape, 0) <= lax.broadcasted_iota(jnp.int32, s.shape, 1)
    s = jnp.where(causal, s, -jnp.inf)
    s_scr[own] = s
    m = jnp.max(s, axis=0, keepdims=True)

    sel = None
    if own > topk:
        blk = lax.broadcasted_iota(jnp.int32, (nb, tq), 0)
        past = blk < own
        gate = jnp.where(past, ext[MOBA_BLOCK:MOBA_BLOCK + nb], -jnp.inf)
        rank = jnp.zeros((nb, tq), jnp.int32)
        for j in range(own):
            gj = gate[j:j + 1, :]
            rank = rank + jnp.where(gj > gate, 1, jnp.where((gj == gate) & (blk > j), 1, 0))
        sel = jnp.where(past & (rank < topk), 1.0, 0.0)

    for n in range(own):
        kn = k_ref[0, n * MOBA_BLOCK:(n + 1) * MOBA_BLOCK, :]
        s = lax.dot_general(kn, q, contract_last, preferred_element_type=F32)
        if sel is not None:
            s = jnp.where(sel[n:n + 1, :] > 0.0, s, -jnp.inf)
        s_scr[n] = s
        m = jnp.maximum(m, jnp.max(s, axis=0, keepdims=True))
    c = HEAD_DIM ** -0.5 * LOG2_E
    l = None
    for n in range(own + 1):
        p = jnp.exp2((s_scr[n] - m) * c)
        bl = jnp.sum(p, axis=0, keepdims=True)
        l = bl if l is None else l + bl
        p_scr[n * MOBA_BLOCK:(n + 1) * MOBA_BLOCK, :] = p.astype(BF16)
    span = (own + 1) * MOBA_BLOCK
    acc = jnp.dot(vt_ref[0, :, :span], p_scr[:span, :], preferred_element_type=F32)
    o_ref[0] = (acc / l).T.astype(o_ref.dtype)


def _attn_kernel(q_ref, k_ref, vt_ref, km_ref, o_ref, s_scr, p_scr, *, topk):
    nb = q_ref.shape[1] // MOBA_BLOCK
    km = km_ref[0, 0]
    for own in range(nb):
        rows = slice(own * MOBA_BLOCK, (own + 1) * MOBA_BLOCK)
        first = own * (own + 1) // 2
        _attn_tile(own, nb, q_ref[0, rows, :], km, k_ref, vt_ref, o_ref.at[:, rows, :],
                   s_scr.at[first:first + own + 1], p_scr.at[first * MOBA_BLOCK:(first + own + 1) * MOBA_BLOCK],
                   topk)


def _attention(q, k, vt, kmean, topk):
    bsz, s, d = q.shape
    nb = s // MOBA_BLOCK
    n_pairs = nb * (nb + 1) // 2
    return pl.pallas_call(
        functools.partial(_attn_kernel, topk=topk),
        grid=(bsz, N_HEADS),
        in_specs=[pl.BlockSpec((1, s, HEAD_DIM), lambda b, h: (b, 0, h)),
                  pl.BlockSpec((1, s, HEAD_DIM), lambda b, h: (b, 0, h)),
                  pl.BlockSpec((1, HEAD_DIM, s), lambda b, h: (b, h, 0)),
                  pl.BlockSpec((1, 1, kmean.shape[2], HEAD_DIM), lambda b, h: (b, h, 0, 0))],
        out_specs=pl.BlockSpec((1, s, HEAD_DIM), lambda b, h: (b, 0, h)),
        out_shape=jax.ShapeDtypeStruct((bsz, s, d), BF16),
        scratch_shapes=[pltpu.VMEM((n_pairs, MOBA_BLOCK, MOBA_BLOCK), F32),
                        pltpu.VMEM((n_pairs * MOBA_BLOCK, MOBA_BLOCK), BF16)],
        compiler_params=_cparams(("arbitrary", "arbitrary")),
        name="moba_attention",
    )(q, k, vt, kmean)


def _oproj_router_kernel(o_ref, x_ref, mod_ref, g_ref, wo_ref, rw_ref, rb_ref, x_out, route_out):
    mod = mod_ref[0]
    mix = jnp.dot(o_ref[0], wo_ref[...], preferred_element_type=F32)
    x = x_ref[0] + mod[2:3] * mix
    x_out[0] = x
    h = _rms_mod(x, g_ref[...], mod[3:4], mod[4:5])

    logits = jnp.dot(h.astype(BF16), rw_ref[...], preferred_element_type=F32) + rb_ref[...]
    lane = lax.broadcasted_iota(jnp.int32, logits.shape, 1)
    logits = jnp.where(lane < N_EXPERTS, logits, -jnp.inf)
    v1 = jnp.max(logits, axis=-1, keepdims=True)
    i1 = jnp.min(jnp.where(logits == v1, lane, LANES), axis=-1, keepdims=True)
    rest = jnp.where(lane == i1, -jnp.inf, logits)
    v2 = jnp.max(rest, axis=-1, keepdims=True)
    i2 = jnp.min(jnp.where(rest == v2, lane, LANES), axis=-1, keepdims=True)
    e2 = jnp.exp(v2 - v1)
    den = 1.0 + e2
    route = jnp.where(lane == 0, i1.astype(F32),
                      jnp.where(lane == 1, i2.astype(F32),
                                jnp.where(lane == 2, 1.0 / den,
                                          jnp.where(lane == 3, e2 / den, 0.0))))
    route_out[0] = route


def _oproj_router(o, x, mod, g, wo, rw, rb, tm):
    bsz, s, d = x.shape
    row = lambda b, j: (0, 0)
    tile = lambda b, j: (b, j, 0)
    return pl.pallas_call(
        _oproj_router_kernel,
        grid=(bsz, s // tm),
        in_specs=[pl.BlockSpec((1, tm, d), tile),
                  pl.BlockSpec((1, tm, d), tile),
                  pl.BlockSpec((1, 6, d), lambda b, j: (b, 0, 0)),
                  pl.BlockSpec((1, d), row),
                  pl.BlockSpec((d, d), row),
                  pl.BlockSpec((d, LANES), row),
                  pl.BlockSpec((1, LANES), row)],
        out_specs=[pl.BlockSpec((1, tm, d), tile),
                   pl.BlockSpec((1, tm, LANES), tile)],
        out_shape=[jax.ShapeDtypeStruct((bsz, s, d), F32),
                   jax.ShapeDtypeStruct((bsz, s, LANES), F32)],
        compiler_params=_cparams(("arbitrary", "arbitrary")),
        name="oproj_router",
    )(o, x, mod, g.reshape(1, d), wo, rw, rb)


DMA_UNROLL = 8


def _pack_bf16_pairs(h):
    half = h.shape[1] // 2
    hb = h.astype(BF16).astype(F32)
    lo = lax.shift_right_logical(pltpu.bitcast(hb[:, :half], jnp.uint32), jnp.uint32(16))
    hi = pltpu.bitcast(hb[:, half:], jnp.uint32) & jnp.uint32(0xFFFF0000)
    return lo | hi


def _unpack_bf16_pairs(w):
    lo = pltpu.bitcast(lax.shift_left(w, jnp.uint32(16)), F32)
    hi = pltpu.bitcast(w & jnp.uint32(0xFFFF0000), F32)
    return lo.astype(BF16), hi.astype(BF16)


def _dispatch_copy(slot_ref, hbuf, hs_hbm, sem, r, k):
    return pltpu.make_async_copy(hbuf.at[pl.ds(r, 1)], hs_hbm.at[pl.ds(slot_ref[0, 0, TOP_K * r + k], 1)], sem)


def _dispatch_kernel(slot_ref, x_ref, mod_ref, g_ref, hs_in, hs_out, hbuf, sem):
    del hs_in
    tm = x_ref.shape[1]
    mod = mod_ref[0]
    hbuf[...] = _pack_bf16_pairs(_rms_mod(x_ref[0], g_ref[...], mod[3:4], mod[4:5]))

    def start(r, c):
        for k in range(TOP_K):
            _dispatch_copy(slot_ref, hbuf, hs_out, sem, r, k).start()
        return c

    def wait(r, c):
        for k in range(TOP_K):
            _dispatch_copy(slot_ref, hbuf, hs_out, sem, r, k).wait()
        return c

    lax.fori_loop(0, tm, start, 0, unroll=DMA_UNROLL)
    lax.fori_loop(0, tm, wait, 0, unroll=DMA_UNROLL)


def _dispatch(slots, x, mod, g, n_slots, tm):
    bsz, s, d = x.shape
    nt = s // tm
    hs0 = jnp.zeros((n_slots, d // 2), jnp.uint32)
    return pl.pallas_call(
        _dispatch_kernel,
        grid=(bsz, nt),
        in_specs=[pl.BlockSpec((1, 1, TOP_K * tm), lambda b, j: (b * nt + j, 0, 0), memory_space=pltpu.SMEM),
                  pl.BlockSpec((1, tm, d), lambda b, j: (b, j, 0)),
                  pl.BlockSpec((1, 6, d), lambda b, j: (b, 0, 0)),
                  pl.BlockSpec((1, d), lambda b, j: (0, 0)),
                  pl.BlockSpec(memory_space=pl.ANY)],
        out_specs=pl.BlockSpec(memory_space=pl.ANY),
        out_shape=jax.ShapeDtypeStruct((n_slots, d // 2), jnp.uint32),
        scratch_shapes=[pltpu.VMEM((tm, d // 2), jnp.uint32), pltpu.SemaphoreType.DMA(())],
        input_output_aliases={4: 0},
        compiler_params=_cparams(("arbitrary", "arbitrary")),
        name="moe_dispatch",
    )(slots.reshape(bsz * nt, 1, TOP_K * tm), x, mod, g.reshape(1, d), hs0)


def _moe_kernel(te_ref, nv_ref, hs_ref, wa_ref, wg_ref, w2_ref, o_ref, h_scr, acc_scr):
    del te_ref
    i = pl.program_id(0)
    j = pl.program_id(1)
    valid = i < nv_ref[0]

    @pl.when(valid & (j == 0))
    def _():
        half = hs_ref.shape[1]
        lo, hi = _unpack_bf16_pairs(hs_ref[...])
        h_scr[:, :half] = lo
        h_scr[:, half:] = hi
        acc_scr[...] = jnp.zeros_like(acc_scr)

    @pl.when(valid)
    def _():
        h = h_scr[...]
        a = jnp.dot(h, wa_ref[0], preferred_element_type=F32)
        g = jnp.dot(h, wg_ref[0], preferred_element_type=F32)
        acc_scr[...] += jnp.dot((_silu(a) * g).astype(BF16), w2_ref[0], preferred_element_type=F32)

    last = j == pl.num_programs(1) - 1

    @pl.when(valid & last)
    def _():
        o_ref[...] = acc_scr[...]

    @pl.when(jnp.logical_not(valid) & last)
    def _():
        o_ref[...] = jnp.zeros_like(o_ref)


def _moe_ffn(tile_expert, n_valid, hs, w13, w2, tm, tf):
    n_slots, half = hs.shape
    d = 2 * half
    f = w2.shape[1]
    nf = f // tf
    nt = n_slots // tm

    def row_idx(i, j, te, nv):
        return (jnp.minimum(i, nv[0] - 1), 0)

    def ff(i, j, nv):
        return jnp.where(i < nv[0], j, nf - 1)

    grid_spec = pltpu.PrefetchScalarGridSpec(
        num_scalar_prefetch=2,
        grid=(nt, nf),
        in_specs=[pl.BlockSpec((tm, half), row_idx),
                  pl.BlockSpec((1, d, tf), lambda i, j, te, nv: (te[i], 0, ff(i, j, nv))),
                  pl.BlockSpec((1, d, tf), lambda i, j, te, nv: (te[i], 0, nf + ff(i, j, nv))),
                  pl.BlockSpec((1, tf, d), lambda i, j, te, nv: (te[i], ff(i, j, nv), 0))],
        out_specs=pl.BlockSpec((tm, d), lambda i, j, te, nv: (i, 0)),
        scratch_shapes=[pltpu.VMEM((tm, d), BF16), pltpu.VMEM((tm, d), F32)],
    )
    return pl.pallas_call(
        _moe_kernel,
        grid_spec=grid_spec,
        out_shape=jax.ShapeDtypeStruct((n_slots, d), F32),
        compiler_params=_cparams(("arbitrary", "arbitrary")),
        name="moe_ffn",
    )(tile_expert, n_valid, hs, w13, w13, w2)


def _combine_copy(slot_ref, ys_hbm, buf, sem, r, k):
    return pltpu.make_async_copy(ys_hbm.at[pl.ds(slot_ref[0, 0, 2 * r + k], 1)],
                                 buf.at[k, pl.ds(r, 1)], sem)


def _combine_kernel(slot_ref, x_ref, route_ref, mod_ref, g_ref, ys_hbm, o_ref, buf, sem):
    tc = x_ref.shape[1]

    def start(r, c):
        for k in range(TOP_K):
            _combine_copy(slot_ref, ys_hbm, buf, sem, r, k).start()
        return c

    def wait(r, c):
        for k in range(TOP_K):
            _combine_copy(slot_ref, ys_hbm, buf, sem, r, k).wait()
        return c

    lax.fori_loop(0, tc, start, 0, unroll=DMA_UNROLL)
    lax.fori_loop(0, tc, wait, 0, unroll=DMA_UNROLL)

    route = route_ref[0]
    y = route[:, 2:3] * buf[0] + route[:, 3:4] * buf[1]
    x = x_ref[0] + mod_ref[0][5:6] * y
    o_ref[0] = (x * lax.rsqrt(jnp.mean(x * x, axis=-1, keepdims=True) + NORM_EPS)) * g_ref[...]


def _combine(slots, x, route, mod, g, ys, tc):
    bsz, s, d = x.shape
    nt = s // tc
    tile = lambda b, j: (b, j, 0)
    return pl.pallas_call(
        _combine_kernel,
        grid=(bsz, nt),
        in_specs=[pl.BlockSpec((1, 1, TOP_K * tc), lambda b, j: (b * nt + j, 0, 0), memory_space=pltpu.SMEM),
                  pl.BlockSpec((1, tc, d), tile),
                  pl.BlockSpec((1, tc, LANES), tile),
                  pl.BlockSpec((1, 6, d), lambda b, j: (b, 0, 0)),
                  pl.BlockSpec((1, d), lambda b, j: (0, 0)),
                  pl.BlockSpec(memory_space=pl.ANY)],
        out_specs=pl.BlockSpec((1, tc, d), tile),
        out_shape=jax.ShapeDtypeStruct((bsz, s, d), F32),
        scratch_shapes=[pltpu.VMEM((TOP_K, tc, d), F32), pltpu.SemaphoreType.DMA(())],
        compiler_params=_cparams(("arbitrary", "arbitrary")),
        name="moe_combine",
    )(slots.reshape(bsz * nt, 1, TOP_K * tc), x, route, mod, g.reshape(1, d), ys)


def _routing_plan(route, tm):
    t = route.shape[0]
    flat_e = route[:, :TOP_K].astype(jnp.int32).reshape(-1)
    onehot = (flat_e[:, None] == jnp.arange(N_EXPERTS, dtype=jnp.int32)[None, :]).astype(jnp.int32)
    csum = jnp.cumsum(onehot, axis=0)
    rank = jnp.sum((csum - onehot) * onehot, axis=1)
    counts = csum[-1]
    padded = ((counts + tm - 1) // tm) * tm
    ends = jnp.cumsum(padded)
    starts = ends - padded
    slots = jnp.sum(starts[None, :] * onehot, axis=1) + rank
    n_tiles = TOP_K * t // tm + N_EXPERTS
    tile_start = jnp.arange(n_tiles, dtype=jnp.int32) * tm
    n_valid = (ends[-1] // tm).astype(jnp.int32)
    tile_expert = jnp.sum((tile_start[:, None] >= ends[None, :]).astype(jnp.int32), axis=1)
    last_expert = jnp.sum((ends[-1] - 1 >= ends).astype(jnp.int32))
    tile_expert = jnp.where(tile_start < ends[-1], tile_expert, last_expert).astype(jnp.int32)
    return slots.astype(jnp.int32), tile_expert, n_valid.reshape(1), n_tiles


def kernel(x, c, positions, ada_w, ada_b, norm1_g, norm2_g, conv_w1, conv_b1, conv_dw_w, conv_dw_b,
           conv_ln_g, conv_ln_b, conv_w2, conv_b2, kv_ada_w, kv_ada_b, kv_norm_g, w_kv, w_q, w_o,
           ffn_w13, ffn_w2, router_w, router_b, moe_w13, moe_w2, final_g):
    bsz, s, d = x.shape
    assert ada_w.shape[0] == 2 and d == N_HEADS * HEAD_DIM and s % MOBA_BLOCK == 0
    nb = s // MOBA_BLOCK
    topk = min(MOBA_TOPK, max(nb - 1, 1))
    tm = 512
    moe_tm = 1024

    mod0 = _ada(c, ada_w[0], ada_b[0]).reshape(bsz, 6, d)
    mod1 = _ada(c, ada_w[1], ada_b[1]).reshape(bsz, 6, d)
    modkv = _ada(c, kv_ada_w, kv_ada_b).reshape(bsz, 2, d)

    x = _conv_mixer(x, mod0, norm1_g[0], conv_w1[0].astype(BF16), conv_b1[0], conv_dw_w[0], conv_dw_b[0],
                    conv_ln_g[0], conv_ln_b[0], conv_w2[0].astype(BF16), conv_b2[0], tm)
    x = _dense_ffn(x, mod0, norm2_g[0], ffn_w13[0].astype(BF16), ffn_w2[0].astype(BF16), tm, 1408)

    inv_freq = ROPE_THETA ** (-jnp.arange(0, ROT_DIM, 2, dtype=F32) / ROT_DIM)
    freq = jnp.concatenate([inv_freq, inv_freq, jnp.zeros((HEAD_DIM - ROT_DIM,), F32)]).reshape(1, HEAD_DIM)
    q, k, vt, kmean = _kvq(x, modkv, mod1, kv_norm_g, norm1_g[1], w_kv.astype(BF16), w_q[0].astype(BF16),
                          positions.reshape(bsz, s, 1), freq, tm)
    kmean = kmean.reshape(bsz, nb, N_HEADS, HEAD_DIM).transpose(0, 2, 1, 3).astype(BF16)
    kmean = jnp.pad(kmean, ((0, 0), (0, 0), (0, -nb % BF16_ROWS), (0, 0)))

    o = _attention(q, k, vt, kmean, topk)
    rw = jnp.zeros((d, LANES), BF16).at[:, :N_EXPERTS].set(router_w[0].astype(BF16))
    rb = jnp.zeros((1, LANES), F32).at[0, :N_EXPERTS].set(router_b[0])
    x, route = _oproj_router(o, x, mod1, norm2_g[1], w_o[0].astype(BF16), rw, rb, tm)

    slots, tile_expert, n_valid, n_tiles = _routing_plan(route.reshape(bsz * s, LANES), moe_tm)
    hs = _dispatch(slots, x, mod1, norm2_g[1], n_tiles * moe_tm, tm)
    ys = _moe_ffn(tile_expert, n_valid, hs, moe_w13[0].astype(BF16), moe_w2[0].astype(BF16), moe_tm, 512)
    return _combine(slots, x, route, mod1, final_g, ys, 256)
```

```python
import functools

import jax
import jax.numpy as jnp
from jax import lax
from jax.experimental import pallas as pl
from jax.experimental.pallas import tpu as pltpu

N_HEADS = 8
HEAD_DIM = 128
ROT_DIM = HEAD_DIM // 4
ROPE_THETA = 500000.0
MOBA_BLOCK = 256
MOBA_TOPK = 3
CONV_WIDTH = 31
N_EXPERTS = 8
TOP_K = 2
NORM_EPS = 1e-6
LOG2_E = 1.4426950408889634

LANES = 128
SUBLANES = 8
BF16_ROWS = 16
CONV_HALO = 32
VMEM_LIMIT = 56 * 1024 * 1024

F32 = jnp.float32
BF16 = jnp.bfloat16
HIGHEST = lax.Precision.HIGHEST


def _cparams(sem):
    return pltpu.CompilerParams(dimension_semantics=sem, vmem_limit_bytes=VMEM_LIMIT)


def _rms_mod(x, g, shift, scale):
    y = x * lax.rsqrt(jnp.mean(x * x, axis=-1, keepdims=True) + NORM_EPS)
    return (y * g) * (1.0 + scale) + shift


def _silu(a):
    return a * jax.nn.sigmoid(a)


def _ada_kernel(c_ref, w_ref, b_ref, o_ref):
    o_ref[...] = jnp.dot(_silu(c_ref[...]), w_ref[...], preferred_element_type=F32,
                         precision=HIGHEST) + b_ref[...]


def _ada(c, w, b):
    bsz, d = c.shape
    n = w.shape[1]
    tn = min(n, 2048)
    return pl.pallas_call(
        _ada_kernel,
        grid=(n // tn,),
        in_specs=[pl.BlockSpec((bsz, d), lambda j: (0, 0)),
                  pl.BlockSpec((d, tn), lambda j: (0, j)),
                  pl.BlockSpec((1, tn), lambda j: (0, j))],
        out_specs=pl.BlockSpec((bsz, tn), lambda j: (0, j)),
        out_shape=jax.ShapeDtypeStruct((bsz, n), F32),
        compiler_params=_cparams(("arbitrary",)),
        name="ada",
    )(c, w, b.reshape(1, n))


CONV_ROWS = 64


def _conv_mixer_kernel(x_ref, mod_ref, g_ref, w1_ref, b1_ref, dw_ref, dwb_ref, lng_ref, lnb_ref,
                       w2_ref, b2_ref, o_ref, ubuf, zbuf):
    tm, d = x_ref.shape[1], x_ref.shape[2]
    j = pl.program_id(1)

    @pl.when(j == 0)
    def _():
        ubuf[0:CONV_HALO, :] = jnp.zeros((CONV_HALO, d), F32)

    @pl.when(j > 0)
    def _():
        ubuf[0:CONV_HALO, :] = ubuf[tm:tm + CONV_HALO, :]

    x = x_ref[0]
    mod = mod_ref[0]
    h = _rms_mod(x, g_ref[...], mod[0:1], mod[1:2]).astype(BF16)
    ag = jnp.dot(h, w1_ref[...], preferred_element_type=F32) + b1_ref[...]
    ubuf[CONV_HALO:CONV_HALO + tm, :] = ag[:, :d] * jax.nn.sigmoid(ag[:, d:])

    lead = CONV_HALO - (CONV_WIDTH - 1)
    win_rows = CONV_ROWS + CONV_HALO

    def chunk(r, carry):
        base = pl.multiple_of(r * CONV_ROWS, CONV_ROWS)
        for lc in range(d // LANES):
            lanes = slice(lc * LANES, (lc + 1) * LANES)
            win = ubuf[pl.ds(base, win_rows), lanes]
            acc = jnp.broadcast_to(dwb_ref[:, lanes], (CONV_ROWS, LANES))
            for phase in range(SUBLANES):
                taps = [k for k in range(CONV_WIDTH) if (lead + k) % SUBLANES == phase]
                if not taps:
                    continue
                shifted = win if phase == 0 else pltpu.roll(win, win_rows - phase, 0)
                for k in taps:
                    off = lead + k - phase
                    acc = acc + dw_ref[k:k + 1, lanes] * shifted[off:off + CONV_ROWS, :]
            zbuf[pl.ds(base, CONV_ROWS), lanes] = acc
        return carry

    lax.fori_loop(0, tm // CONV_ROWS, chunk, 0)

    z = zbuf[...]
    mu = jnp.mean(z, axis=-1, keepdims=True)
    zc = z - mu
    var = jnp.mean(zc * zc, axis=-1, keepdims=True)
    zn = (zc * lax.rsqrt(var + NORM_EPS)) * lng_ref[...] + lnb_ref[...]
    mix = jnp.dot(_silu(zn).astype(BF16), w2_ref[...], preferred_element_type=F32) + b2_ref[...]
    o_ref[0] = x + mod[2:3] * mix


def _conv_mixer(x, mod, g, w1, b1, dw, dwb, lng, lnb, w2, b2, tm):
    bsz, s, d = x.shape
    row = lambda b, j: (0, 0)
    return pl.pallas_call(
        _conv_mixer_kernel,
        grid=(bsz, s // tm),
        in_specs=[pl.BlockSpec((1, tm, d), lambda b, j: (b, j, 0)),
                  pl.BlockSpec((1, 6, d), lambda b, j: (b, 0, 0)),
                  pl.BlockSpec((1, d), row),
                  pl.BlockSpec((d, 2 * d), row),
                  pl.BlockSpec((1, 2 * d), row),
                  pl.BlockSpec((CONV_WIDTH, d), row),
                  pl.BlockSpec((1, d), row),
                  pl.BlockSpec((1, d), row),
                  pl.BlockSpec((1, d), row),
                  pl.BlockSpec((d, d), row),
                  pl.BlockSpec((1, d), row)],
        out_specs=pl.BlockSpec((1, tm, d), lambda b, j: (b, j, 0)),
        out_shape=jax.ShapeDtypeStruct((bsz, s, d), F32),
        scratch_shapes=[pltpu.VMEM((tm + CONV_HALO, d), F32), pltpu.VMEM((tm, d), F32)],
        compiler_params=_cparams(("arbitrary", "arbitrary")),
        name="conv_mixer",
    )(x, mod, g.reshape(1, d), w1, b1.reshape(1, 2 * d), dw, dwb.reshape(1, d), lng.reshape(1, d),
      lnb.reshape(1, d), w2, b2.reshape(1, d))


def _ffn_kernel(x_ref, mod_ref, g_ref, w13_ref, w2_ref, o_ref, *, chunk):
    f = w2_ref.shape[0]
    x = x_ref[0]
    mod = mod_ref[0]
    h = _rms_mod(x, g_ref[...], mod[3:4], mod[4:5]).astype(BF16)
    acc = None
    for c0 in range(0, f, chunk):
        c1 = min(c0 + chunk, f)
        a = jnp.dot(h, w13_ref[:, c0:c1], preferred_element_type=F32)
        g = jnp.dot(h, w13_ref[:, f + c0:f + c1], preferred_element_type=F32)
        part = jnp.dot((_silu(a) * g).astype(BF16), w2_ref[c0:c1, :], preferred_element_type=F32)
        acc = part if acc is None else acc + part
    o_ref[0] = x + mod[5:6] * acc


def _dense_ffn(x, mod, g, w13, w2, tm, chunk):
    bsz, s, d = x.shape
    f = w2.shape[0]
    const = lambda b, i: (0, 0)
    return pl.pallas_call(
        functools.partial(_ffn_kernel, chunk=chunk),
        grid=(bsz, s // tm),
        in_specs=[pl.BlockSpec((1, tm, d), lambda b, i: (b, i, 0)),
                  pl.BlockSpec((1, 6, d), lambda b, i: (b, 0, 0)),
                  pl.BlockSpec((1, d), const),
                  pl.BlockSpec((d, 2 * f), const, pipeline_mode=pl.Buffered(1)),
                  pl.BlockSpec((f, d), const, pipeline_mode=pl.Buffered(1))],
        out_specs=pl.BlockSpec((1, tm, d), lambda b, i: (b, i, 0)),
        out_shape=jax.ShapeDtypeStruct((bsz, s, d), F32),
        compiler_params=_cparams(("arbitrary", "arbitrary")),
        name="dense_ffn",
    )(x, mod, g.reshape(1, d), w13, w2)


def _rotary_heads(y, cos, sin_signed, low_half):
    outs = []
    for hh in range(y.shape[1] // HEAD_DIM):
        ys = y[:, hh * HEAD_DIM:(hh + 1) * HEAD_DIM]
        partner = jnp.where(low_half, pltpu.roll(ys, HEAD_DIM - ROT_DIM // 2, 1),
                            pltpu.roll(ys, ROT_DIM // 2, 1))
        outs.append(ys * cos + partner * sin_signed)
    return outs


def _kvq_kernel(x_ref, modkv_ref, mod_ref, gkv_ref, gq_ref, wkv_ref, wq_ref, pos_ref, freq_ref,
                q_ref, k_ref, vt_ref, km_ref):
    tm, d = x_ref.shape[1], x_ref.shape[2]
    modkv = modkv_ref[0]
    mod = mod_ref[0]
    lane = lax.broadcasted_iota(jnp.int32, (MOBA_BLOCK, HEAD_DIM), 1)
    low_half = lane < ROT_DIM // 2

    for r in range(tm // MOBA_BLOCK):
        rows = slice(r * MOBA_BLOCK, (r + 1) * MOBA_BLOCK)
        x = x_ref[0, rows, :]
        xn = x * lax.rsqrt(jnp.mean(x * x, axis=-1, keepdims=True) + NORM_EPS)
        hkv = ((xn * gkv_ref[...]) * (1.0 + modkv[1:2]) + modkv[0:1]).astype(BF16)
        hq = ((xn * gq_ref[...]) * (1.0 + mod[1:2]) + mod[0:1]).astype(BF16)
        kv = jnp.dot(hkv, wkv_ref[...], preferred_element_type=F32)
        qf = jnp.dot(hq, wq_ref[...], preferred_element_type=F32)

        ang = pos_ref[0, rows, :].astype(F32) * freq_ref[...]
        cos = jnp.cos(ang)
        sin = jnp.sin(ang)
        sin_signed = jnp.where(low_half, -sin, sin)

        vt_ref[0, :, rows] = kv[:, d:].T.astype(BF16)
        k_heads = _rotary_heads(kv[:, :d], cos, sin_signed, low_half)
        q_heads = _rotary_heads(qf, cos, sin_signed, low_half)
        for hh in range(N_HEADS):
            hs = slice(hh * HEAD_DIM, (hh + 1) * HEAD_DIM)
            q_ref[0, rows, hs] = q_heads[hh].astype(BF16)
            k_ref[0, rows, hs] = k_heads[hh].astype(BF16)
            km_ref[0, r, :, hs] = jnp.mean(k_heads[hh], axis=0, keepdims=True)


def _kvq(x, modkv, mod, gkv, gq, wkv, wq, pos, freq, tm):
    bsz, s, d = x.shape
    nb = s // MOBA_BLOCK
    row = lambda b, j: (0, 0)
    act = jax.ShapeDtypeStruct((bsz, s, d), BF16)
    return pl.pallas_call(
        _kvq_kernel,
        grid=(bsz, s // tm),
        in_specs=[pl.BlockSpec((1, tm, d), lambda b, j: (b, j, 0)),
                  pl.BlockSpec((1, 2, d), lambda b, j: (b, 0, 0)),
                  pl.BlockSpec((1, 6, d), lambda b, j: (b, 0, 0)),
                  pl.BlockSpec((1, d), row),
                  pl.BlockSpec((1, d), row),
                  pl.BlockSpec((d, 2 * d), row),
                  pl.BlockSpec((d, d), row),
                  pl.BlockSpec((1, tm, 1), lambda b, j: (b, j, 0)),
                  pl.BlockSpec((1, HEAD_DIM), row)],
        out_specs=[pl.BlockSpec((1, tm, d), lambda b, j: (b, j, 0)),
                   pl.BlockSpec((1, tm, d), lambda b, j: (b, j, 0)),
                   pl.BlockSpec((1, d, tm), lambda b, j: (b, 0, j)),
                   pl.BlockSpec((1, tm // MOBA_BLOCK, 1, d), lambda b, j: (b, j, 0, 0))],
        out_shape=[act, act, jax.ShapeDtypeStruct((bsz, d, s), BF16),
                   jax.ShapeDtypeStruct((bsz, nb, 1, d), F32)],
        compiler_params=_cparams(("arbitrary", "arbitrary")),
        name="kvq_proj",
    )(x, modkv, mod, gkv.reshape(1, d), gq.reshape(1, d), wkv, wq, pos, freq)


def _attn_scores(own, nb, q, km, k_ref, s_scr, topk):
    contract_last = (((1,), (1,)), ((), ()))
    tq = q.shape[0]

    k_own = k_ref[0, own * MOBA_BLOCK:(own + 1) * MOBA_BLOCK, :]
    ext = lax.dot_general(jnp.concatenate([k_own, km], axis=0), q, contract_last,
                          preferred_element_type=F32)
    s = ext[:MOBA_BLOCK]
    causal = lax.broadcasted_iota(jnp.int32, s.shape, 0) <= lax.broadcasted_iota(jnp.int32, s.shape, 1)
    s = jnp.where(causal, s, -jnp.inf)
    s_scr[own] = s
    m = jnp.max(s, axis=0, keepdims=True)

    sel = None
    if own > topk:
        blk = lax.broadcasted_iota(jnp.int32, (nb, tq), 0)
        past = blk < own
        gate = jnp.where(past, ext[MOBA_BLOCK:MOBA_BLOCK + nb], -jnp.inf)
        rank = jnp.zeros((nb, tq), jnp.int32)
        for j in range(own):
            gj = gate[j:j + 1, :]
            rank = rank + jnp.where(gj > gate, 1, jnp.where((gj == gate) & (blk > j), 1, 0))
        sel = jnp.where(past & (rank < topk), 1.0, 0.0)

    for n in range(own):
        kn = k_ref[0, n * MOBA_BLOCK:(n + 1) * MOBA_BLOCK, :]
        s = lax.dot_general(kn, q, contract_last, preferred_element_type=F32)
        if sel is not None:
            s = jnp.where(sel[n:n + 1, :] > 0.0, s, -jnp.inf)
        s_scr[n] = s
        m = jnp.maximum(m, jnp.max(s, axis=0, keepdims=True))
    return m


def _attn_output(own, m, vt_ref, o_ref, s_scr, p_scr):
    c = HEAD_DIM ** -0.5 * LOG2_E
    l = None
    for n in range(own + 1):
        p = jnp.exp2((s_scr[n] - m) * c)
        bl = jnp.sum(p, axis=0, keepdims=True)
        l = bl if l is None else l + bl
        p_scr[n * MOBA_BLOCK:(n + 1) * MOBA_BLOCK, :] = p.astype(BF16)
    span = (own + 1) * MOBA_BLOCK
    acc = jnp.dot(vt_ref[0, :, :span], p_scr[:span, :], preferred_element_type=F32)
    o_ref[0] = (acc / l).T.astype(o_ref.dtype)


def _attn_kernel(q_ref, k_ref, vt_ref, km_ref, o_ref, s_scr, p_scr, *, topk):
    nb = q_ref.shape[1] // MOBA_BLOCK
    km = km_ref[0, 0]

    def rows(own):
        return slice(own * MOBA_BLOCK, (own + 1) * MOBA_BLOCK)

    def blocks(own):
        first = own * (own + 1) // 2
        return slice(first, first + own + 1), slice(first * MOBA_BLOCK, (first + own + 1) * MOBA_BLOCK)

    def scores(own):
        return _attn_scores(own, nb, q_ref[0, rows(own), :], km, k_ref, s_scr.at[blocks(own)[0]], topk)

    m = scores(0)
    for own in range(nb):
        m_next = scores(own + 1) if own + 1 < nb else None
        _attn_output(own, m, vt_ref, o_ref.at[:, rows(own), :], s_scr.at[blocks(own)[0]], p_scr.at[blocks(own)[1]])
        m = m_next


def _attention(q, k, vt, kmean, topk):
    bsz, s, d = q.shape
    nb = s // MOBA_BLOCK
    n_pairs = nb * (nb + 1) // 2
    return pl.pallas_call(
        functools.partial(_attn_kernel, topk=topk),
        grid=(bsz, N_HEADS),
        in_specs=[pl.BlockSpec((1, s, HEAD_DIM), lambda b, h: (b, 0, h)),
                  pl.BlockSpec((1, s, HEAD_DIM), lambda b, h: (b, 0, h)),
                  pl.BlockSpec((1, HEAD_DIM, s), lambda b, h: (b, h, 0)),
                  pl.BlockSpec((1, 1, kmean.shape[2], HEAD_DIM), lambda b, h: (b, h, 0, 0))],
        out_specs=pl.BlockSpec((1, s, HEAD_DIM), lambda b, h: (b, 0, h)),
        out_shape=jax.ShapeDtypeStruct((bsz, s, d), BF16),
        scratch_shapes=[pltpu.VMEM((n_pairs, MOBA_BLOCK, MOBA_BLOCK), F32),
                        pltpu.VMEM((n_pairs * MOBA_BLOCK, MOBA_BLOCK), BF16)],
        compiler_params=_cparams(("arbitrary", "arbitrary")),
        name="moba_attention",
    )(q, k, vt, kmean)


def _oproj_router_kernel(o_ref, x_ref, mod_ref, g_ref, wo_ref, rw_ref, rb_ref, x_out, route_out):
    mod = mod_ref[0]
    mix = jnp.dot(o_ref[0], wo_ref[...], preferred_element_type=F32)
    x = x_ref[0] + mod[2:3] * mix
    x_out[0] = x
    h = _rms_mod(x, g_ref[...], mod[3:4], mod[4:5])

    logits = jnp.dot(h.astype(BF16), rw_ref[...], preferred_element_type=F32) + rb_ref[...]
    lane = lax.broadcasted_iota(jnp.int32, logits.shape, 1)
    logits = jnp.where(lane < N_EXPERTS, logits, -jnp.inf)
    v1 = jnp.max(logits, axis=-1, keepdims=True)
    i1 = jnp.min(jnp.where(logits == v1, lane, LANES), axis=-1, keepdims=True)
    rest = jnp.where(lane == i1, -jnp.inf, logits)
    v2 = jnp.max(rest, axis=-1, keepdims=True)
    i2 = jnp.min(jnp.where(rest == v2, lane, LANES), axis=-1, keepdims=True)
    e2 = jnp.exp(v2 - v1)
    den = 1.0 + e2
    route = jnp.where(lane == 0, i1.astype(F32),
                      jnp.where(lane == 1, i2.astype(F32),
                                jnp.where(lane == 2, 1.0 / den,
                                          jnp.where(lane == 3, e2 / den, 0.0))))
    route_out[0] = route


def _oproj_router(o, x, mod, g, wo, rw, rb, tm):
    bsz, s, d = x.shape
    row = lambda b, j: (0, 0)
    tile = lambda b, j: (b, j, 0)
    return pl.pallas_call(
        _oproj_router_kernel,
        grid=(bsz, s // tm),
        in_specs=[pl.BlockSpec((1, tm, d), tile),
                  pl.BlockSpec((1, tm, d), tile),
                  pl.BlockSpec((1, 6, d), lambda b, j: (b, 0, 0)),
                  pl.BlockSpec((1, d), row),
                  pl.BlockSpec((d, d), row),
                  pl.BlockSpec((d, LANES), row),
                  pl.BlockSpec((1, LANES), row)],
        out_specs=[pl.BlockSpec((1, tm, d), tile),
                   pl.BlockSpec((1, tm, LANES), tile)],
        out_shape=[jax.ShapeDtypeStruct((bsz, s, d), F32),
                   jax.ShapeDtypeStruct((bsz, s, LANES), F32)],
        compiler_params=_cparams(("arbitrary", "arbitrary")),
        name="oproj_router",
    )(o, x, mod, g.reshape(1, d), wo, rw, rb)


DMA_UNROLL = 8


def _pack_bf16_pairs(h):
    half = h.shape[1] // 2
    hb = h.astype(BF16).astype(F32)
    lo = lax.shift_right_logical(pltpu.bitcast(hb[:, :half], jnp.uint32), jnp.uint32(16))
    hi = pltpu.bitcast(hb[:, half:], jnp.uint32) & jnp.uint32(0xFFFF0000)
    return lo | hi


def _unpack_bf16_pairs(w):
    lo = pltpu.bitcast(lax.shift_left(w, jnp.uint32(16)), F32)
    hi = pltpu.bitcast(w & jnp.uint32(0xFFFF0000), F32)
    return lo.astype(BF16), hi.astype(BF16)


def _dispatch_copy(slot_ref, hbuf, hs_hbm, sem, r, k):
    return pltpu.make_async_copy(hbuf.at[pl.ds(r, 1)], hs_hbm.at[pl.ds(slot_ref[0, 0, TOP_K * r + k], 1)], sem)


def _dispatch_kernel(slot_ref, x_ref, mod_ref, g_ref, hs_in, hs_out, hbuf, sem):
    del hs_in
    tm = x_ref.shape[1]
    mod = mod_ref[0]
    hbuf[...] = _pack_bf16_pairs(_rms_mod(x_ref[0], g_ref[...], mod[3:4], mod[4:5]))

    def start(r, c):
        for k in range(TOP_K):
            _dispatch_copy(slot_ref, hbuf, hs_out, sem, r, k).start()
        return c

    def wait(r, c):
        for k in range(TOP_K):
            _dispatch_copy(slot_ref, hbuf, hs_out, sem, r, k).wait()
        return c

    lax.fori_loop(0, tm, start, 0, unroll=DMA_UNROLL)
    lax.fori_loop(0, tm, wait, 0, unroll=DMA_UNROLL)


def _dispatch(slots, x, mod, g, n_slots, tm):
    bsz, s, d = x.shape
    nt = s // tm
    hs0 = jnp.zeros((n_slots, d // 2), jnp.uint32)
    return pl.pallas_call(
        _dispatch_kernel,
        grid=(bsz, nt),
        in_specs=[pl.BlockSpec((1, 1, TOP_K * tm), lambda b, j: (b * nt + j, 0, 0), memory_space=pltpu.SMEM),
                  pl.BlockSpec((1, tm, d), lambda b, j: (b, j, 0)),
                  pl.BlockSpec((1, 6, d), lambda b, j: (b, 0, 0)),
                  pl.BlockSpec((1, d), lambda b, j: (0, 0)),
                  pl.BlockSpec(memory_space=pl.ANY)],
        out_specs=pl.BlockSpec(memory_space=pl.ANY),
        out_shape=jax.ShapeDtypeStruct((n_slots, d // 2), jnp.uint32),
        scratch_shapes=[pltpu.VMEM((tm, d // 2), jnp.uint32), pltpu.SemaphoreType.DMA(())],
        input_output_aliases={4: 0},
        compiler_params=_cparams(("arbitrary", "arbitrary")),
        name="moe_dispatch",
    )(slots.reshape(bsz * nt, 1, TOP_K * tm), x, mod, g.reshape(1, d), hs0)


def _moe_kernel(te_ref, nv_ref, hs_ref, wa_ref, wg_ref, w2_ref, o_ref, h_scr, acc_scr):
    del te_ref
    i = pl.program_id(0)
    j = pl.program_id(1)
    valid = i < nv_ref[0]

    @pl.when(valid & (j == 0))
    def _():
        half = hs_ref.shape[1]
        lo, hi = _unpack_bf16_pairs(hs_ref[...])
        h_scr[:, :half] = lo
        h_scr[:, half:] = hi
        acc_scr[...] = jnp.zeros_like(acc_scr)

    @pl.when(valid)
    def _():
        h = h_scr[...]
        a = jnp.dot(h, wa_ref[0], preferred_element_type=F32)
        g = jnp.dot(h, wg_ref[0], preferred_element_type=F32)
        acc_scr[...] += jnp.dot((_silu(a) * g).astype(BF16), w2_ref[0], preferred_element_type=F32)

    last = j == pl.num_programs(1) - 1

    @pl.when(valid & last)
    def _():
        o_ref[...] = acc_scr[...]

    @pl.when(jnp.logical_not(valid) & last)
    def _():
        o_ref[...] = jnp.zeros_like(o_ref)


def _moe_ffn(tile_expert, n_valid, hs, w13, w2, tm, tf):
    n_slots, half = hs.shape
    d = 2 * half
    f = w2.shape[1]
    nf = f // tf
    nt = n_slots // tm

    def row_idx(i, j, te, nv):
        return (jnp.minimum(i, nv[0] - 1), 0)

    def ff(i, j, nv):
        return jnp.where(i < nv[0], j, nf - 1)

    grid_spec = pltpu.PrefetchScalarGridSpec(
        num_scalar_prefetch=2,
        grid=(nt, nf),
        in_specs=[pl.BlockSpec((tm, half), row_idx),
                  pl.BlockSpec((1, d, tf), lambda i, j, te, nv: (te[i], 0, ff(i, j, nv))),
                  pl.BlockSpec((1, d, tf), lambda i, j, te, nv: (te[i], 0, nf + ff(i, j, nv))),
                  pl.BlockSpec((1, tf, d), lambda i, j, te, nv: (te[i], ff(i, j, nv), 0))],
        out_specs=pl.BlockSpec((tm, d), lambda i, j, te, nv: (i, 0)),
        scratch_shapes=[pltpu.VMEM((tm, d), BF16), pltpu.VMEM((tm, d), F32)],
    )
    return pl.pallas_call(
        _moe_kernel,
        grid_spec=grid_spec,
        out_shape=jax.ShapeDtypeStruct((n_slots, d), F32),
        compiler_params=_cparams(("arbitrary", "arbitrary")),
        name="moe_ffn",
    )(tile_expert, n_valid, hs, w13, w13, w2)


def _combine_copy(slot_ref, ys_hbm, buf, sems, par, r, k):
    return pltpu.make_async_copy(ys_hbm.at[pl.ds(slot_ref[0, 0, TOP_K * r + k], 1)],
                                 buf.at[par, k, pl.ds(r, 1)], sems.at[par])


def _combine_kernel(slot_ref, next_slot_ref, x_ref, route_ref, mod_ref, g_ref, ys_hbm, o_ref, buf, sems):
    tc = x_ref.shape[1]
    g = pl.program_id(0) * pl.num_programs(1) + pl.program_id(1)
    n_steps = pl.num_programs(0) * pl.num_programs(1)
    par = lax.rem(g, 2)

    def fetch(slots, parity):
        def start(r, c):
            for k in range(TOP_K):
                _combine_copy(slots, ys_hbm, buf, sems, parity, r, k).start()
            return c
        lax.fori_loop(0, tc, start, 0, unroll=DMA_UNROLL)

    @pl.when(g == 0)
    def _():
        fetch(slot_ref, par)

    @pl.when(g + 1 < n_steps)
    def _():
        fetch(next_slot_ref, 1 - par)

    def wait(r, c):
        for k in range(TOP_K):
            _combine_copy(slot_ref, ys_hbm, buf, sems, par, r, k).wait()
        return c

    lax.fori_loop(0, tc, wait, 0, unroll=DMA_UNROLL)

    route = route_ref[0]
    y = route[:, 2:3] * buf[par, 0] + route[:, 3:4] * buf[par, 1]
    x = x_ref[0] + mod_ref[0][5:6] * y
    o_ref[0] = (x * lax.rsqrt(jnp.mean(x * x, axis=-1, keepdims=True) + NORM_EPS)) * g_ref[...]


def _combine(slots, x, route, mod, g, ys, tc):
    bsz, s, d = x.shape
    nt = s // tc
    tile = lambda b, j: (b, j, 0)
    slots = slots.reshape(bsz * nt, 1, TOP_K * tc)
    return pl.pallas_call(
        _combine_kernel,
        grid=(bsz, nt),
        in_specs=[pl.BlockSpec((1, 1, TOP_K * tc), lambda b, j: (b * nt + j, 0, 0), memory_space=pltpu.SMEM),
                  pl.BlockSpec((1, 1, TOP_K * tc), lambda b, j: (jnp.minimum(b * nt + j + 1, bsz * nt - 1), 0, 0),
                               memory_space=pltpu.SMEM),
                  pl.BlockSpec((1, tc, d), tile),
                  pl.BlockSpec((1, tc, LANES), tile),
                  pl.BlockSpec((1, 6, d), lambda b, j: (b, 0, 0)),
                  pl.BlockSpec((1, d), lambda b, j: (0, 0)),
                  pl.BlockSpec(memory_space=pl.ANY)],
        out_specs=pl.BlockSpec((1, tc, d), tile),
        out_shape=jax.ShapeDtypeStruct((bsz, s, d), F32),
        scratch_shapes=[pltpu.VMEM((2, TOP_K, tc, d), F32), pltpu.SemaphoreType.DMA((2,))],
        compiler_params=_cparams(("arbitrary", "arbitrary")),
        name="moe_combine",
    )(slots, slots, x, route, mod, g.reshape(1, d), ys)


def _routing_plan(route, tm):
    t = route.shape[0]
    flat_e = route[:, :TOP_K].astype(jnp.int32).reshape(-1)
    onehot = (flat_e[:, None] == jnp.arange(N_EXPERTS, dtype=jnp.int32)[None, :]).astype(jnp.int32)
    csum = jnp.cumsum(onehot, axis=0)
    rank = jnp.sum((csum - onehot) * onehot, axis=1)
    counts = csum[-1]
    padded = ((counts + tm - 1) // tm) * tm
    ends = jnp.cumsum(padded)
    starts = ends - padded
    slots = jnp.sum(starts[None, :] * onehot, axis=1) + rank
    n_tiles = TOP_K * t // tm + N_EXPERTS
    tile_start = jnp.arange(n_tiles, dtype=jnp.int32) * tm
    n_valid = (ends[-1] // tm).astype(jnp.int32)
    tile_expert = jnp.sum((tile_start[:, None] >= ends[None, :]).astype(jnp.int32), axis=1)
    last_expert = jnp.sum((ends[-1] - 1 >= ends).astype(jnp.int32))
    tile_expert = jnp.where(tile_start < ends[-1], tile_expert, last_expert).astype(jnp.int32)
    return slots.astype(jnp.int32), tile_expert, n_valid.reshape(1), n_tiles


def kernel(x, c, positions, ada_w, ada_b, norm1_g, norm2_g, conv_w1, conv_b1, conv_dw_w, conv_dw_b,
           conv_ln_g, conv_ln_b, conv_w2, conv_b2, kv_ada_w, kv_ada_b, kv_norm_g, w_kv, w_q, w_o,
           ffn_w13, ffn_w2, router_w, router_b, moe_w13, moe_w2, final_g):
    bsz, s, d = x.shape
    assert ada_w.shape[0] == 2 and d == N_HEADS * HEAD_DIM and s % MOBA_BLOCK == 0
    nb = s // MOBA_BLOCK
    topk = min(MOBA_TOPK, max(nb - 1, 1))
    tm = 512
    moe_tm = 1024

    mod0 = _ada(c, ada_w[0], ada_b[0]).reshape(bsz, 6, d)
    mod1 = _ada(c, ada_w[1], ada_b[1]).reshape(bsz, 6, d)
    modkv = _ada(c, kv_ada_w, kv_ada_b).reshape(bsz, 2, d)

    x = _conv_mixer(x, mod0, norm1_g[0], conv_w1[0].astype(BF16), conv_b1[0], conv_dw_w[0], conv_dw_b[0],
                    conv_ln_g[0], conv_ln_b[0], conv_w2[0].astype(BF16), conv_b2[0], tm)
    x = _dense_ffn(x, mod0, norm2_g[0], ffn_w13[0].astype(BF16), ffn_w2[0].astype(BF16), tm, 512)

    inv_freq = ROPE_THETA ** (-jnp.arange(0, ROT_DIM, 2, dtype=F32) / ROT_DIM)
    freq = jnp.concatenate([inv_freq, inv_freq, jnp.zeros((HEAD_DIM - ROT_DIM,), F32)]).reshape(1, HEAD_DIM)
    q, k, vt, kmean = _kvq(x, modkv, mod1, kv_norm_g, norm1_g[1], w_kv.astype(BF16), w_q[0].astype(BF16),
                          positions.reshape(bsz, s, 1), freq, tm)
    kmean = kmean.reshape(bsz, nb, N_HEADS, HEAD_DIM).transpose(0, 2, 1, 3).astype(BF16)
    kmean = jnp.pad(kmean, ((0, 0), (0, 0), (0, -nb % BF16_ROWS), (0, 0)))

    o = _attention(q, k, vt, kmean, topk)
    rw = jnp.zeros((d, LANES), BF16).at[:, :N_EXPERTS].set(router_w[0].astype(BF16))
    rb = jnp.zeros((1, LANES), F32).at[0, :N_EXPERTS].set(router_b[0])
    x, route = _oproj_router(o, x, mod1, norm2_g[1], w_o[0].astype(BF16), rw, rb, tm)

    slots, tile_expert, n_valid, n_tiles = _routing_plan(route.reshape(bsz * s, LANES), moe_tm)
    hs = _dispatch(slots, x, mod1, norm2_g[1], n_tiles * moe_tm, tm)
    ys = _moe_ffn(tile_expert, n_valid, hs, moe_w13[0].astype(BF16), moe_w2[0].astype(BF16), moe_tm, 512)
    return _combine(slots, x, route, mod1, final_g, ys, tm)
```

```python
import functools

import jax
import jax.numpy as jnp
from jax import lax
from jax.experimental import pallas as pl
from jax.experimental.pallas import tpu as pltpu

N_HEADS = 8
HEAD_DIM = 128
ROT_DIM = HEAD_DIM // 4
ROPE_THETA = 500000.0
MOBA_BLOCK = 256
MOBA_TOPK = 3
CONV_WIDTH = 31
N_EXPERTS = 8
TOP_K = 2
NORM_EPS = 1e-6
LOG2_E = 1.4426950408889634

LANES = 128
SUBLANES = 8
BF16_ROWS = 16
CONV_HALO = 32
VMEM_LIMIT = 56 * 1024 * 1024

F32 = jnp.float32
BF16 = jnp.bfloat16
HIGHEST = lax.Precision.HIGHEST


def _cparams(sem):
    return pltpu.CompilerParams(dimension_semantics=sem, vmem_limit_bytes=VMEM_LIMIT)


def _rms_mod(x, g, shift, scale):
    y = x * lax.rsqrt(jnp.mean(x * x, axis=-1, keepdims=True) + NORM_EPS)
    return (y * g) * (1.0 + scale) + shift


def _silu(a):
    return a * jax.nn.sigmoid(a)


def _ada_kernel(c_ref, w_ref, b_ref, o_ref):
    o_ref[...] = jnp.dot(_silu(c_ref[...]), w_ref[...], preferred_element_type=F32,
                         precision=HIGHEST) + b_ref[...]


def _ada(c, w, b):
    bsz, d = c.shape
    n = w.shape[1]
    tn = min(n, 2048)
    return pl.pallas_call(
        _ada_kernel,
        grid=(n // tn,),
        in_specs=[pl.BlockSpec((bsz, d), lambda j: (0, 0)),
                  pl.BlockSpec((d, tn), lambda j: (0, j)),
                  pl.BlockSpec((1, tn), lambda j: (0, j))],
        out_specs=pl.BlockSpec((bsz, tn), lambda j: (0, j)),
        out_shape=jax.ShapeDtypeStruct((bsz, n), F32),
        compiler_params=_cparams(("arbitrary",)),
        name="ada",
    )(c, w, b.reshape(1, n))


CONV_ROWS = 64


def _conv_mixer_kernel(x_ref, mod_ref, g_ref, w1_ref, b1_ref, dw_ref, dwb_ref, lng_ref, lnb_ref,
                       w2_ref, b2_ref, o_ref, ubuf, zbuf):
    tm, d = x_ref.shape[1], x_ref.shape[2]
    j = pl.program_id(1)

    @pl.when(j == 0)
    def _():
        ubuf[0:CONV_HALO, :] = jnp.zeros((CONV_HALO, d), F32)

    @pl.when(j > 0)
    def _():
        ubuf[0:CONV_HALO, :] = ubuf[tm:tm + CONV_HALO, :]

    x = x_ref[0]
    mod = mod_ref[0]
    h = _rms_mod(x, g_ref[...], mod[0:1], mod[1:2]).astype(BF16)
    ag = jnp.dot(h, w1_ref[...], preferred_element_type=F32) + b1_ref[...]
    ubuf[CONV_HALO:CONV_HALO + tm, :] = ag[:, :d] * jax.nn.sigmoid(ag[:, d:])

    lead = CONV_HALO - (CONV_WIDTH - 1)
    win_rows = CONV_ROWS + CONV_HALO

    def chunk(r, carry):
        base = pl.multiple_of(r * CONV_ROWS, CONV_ROWS)
        for lc in range(d // LANES):
            lanes = slice(lc * LANES, (lc + 1) * LANES)
            win = ubuf[pl.ds(base, win_rows), lanes]
            acc = jnp.broadcast_to(dwb_ref[:, lanes], (CONV_ROWS, LANES))
            for phase in range(SUBLANES):
                taps = [k for k in range(CONV_WIDTH) if (lead + k) % SUBLANES == phase]
                if not taps:
                    continue
                shifted = win if phase == 0 else pltpu.roll(win, win_rows - phase, 0)
                for k in taps:
                    off = lead + k - phase
                    acc = acc + dw_ref[k:k + 1, lanes] * shifted[off:off + CONV_ROWS, :]
            zbuf[pl.ds(base, CONV_ROWS), lanes] = acc
        return carry

    lax.fori_loop(0, tm // CONV_ROWS, chunk, 0)

    z = zbuf[...]
    mu = jnp.mean(z, axis=-1, keepdims=True)
    zc = z - mu
    var = jnp.mean(zc * zc, axis=-1, keepdims=True)
    zn = (zc * lax.rsqrt(var + NORM_EPS)) * lng_ref[...] + lnb_ref[...]
    mix = jnp.dot(_silu(zn).astype(BF16), w2_ref[...], preferred_element_type=F32) + b2_ref[...]
    o_ref[0] = x + mod[2:3] * mix


def _conv_mixer(x, mod, g, w1, b1, dw, dwb, lng, lnb, w2, b2, tm):
    bsz, s, d = x.shape
    row = lambda b, j: (0, 0)
    return pl.pallas_call(
        _conv_mixer_kernel,
        grid=(bsz, s // tm),
        in_specs=[pl.BlockSpec((1, tm, d), lambda b, j: (b, j, 0)),
                  pl.BlockSpec((1, 6, d), lambda b, j: (b, 0, 0)),
                  pl.BlockSpec((1, d), row),
                  pl.BlockSpec((d, 2 * d), row),
                  pl.BlockSpec((1, 2 * d), row),
                  pl.BlockSpec((CONV_WIDTH, d), row),
                  pl.BlockSpec((1, d), row),
                  pl.BlockSpec((1, d), row),
                  pl.BlockSpec((1, d), row),
                  pl.BlockSpec((d, d), row),
                  pl.BlockSpec((1, d), row)],
        out_specs=pl.BlockSpec((1, tm, d), lambda b, j: (b, j, 0)),
        out_shape=jax.ShapeDtypeStruct((bsz, s, d), F32),
        scratch_shapes=[pltpu.VMEM((tm + CONV_HALO, d), F32), pltpu.VMEM((tm, d), F32)],
        compiler_params=_cparams(("arbitrary", "arbitrary")),
        name="conv_mixer",
    )(x, mod, g.reshape(1, d), w1, b1.reshape(1, 2 * d), dw, dwb.reshape(1, d), lng.reshape(1, d),
      lnb.reshape(1, d), w2, b2.reshape(1, d))


def _ffn_kernel(x_ref, mod_ref, g_ref, w13_ref, w2_ref, o_ref, *, chunk):
    f = w2_ref.shape[0]
    x = x_ref[0]
    mod = mod_ref[0]
    h = _rms_mod(x, g_ref[...], mod[3:4], mod[4:5]).astype(BF16)
    acc = None
    for c0 in range(0, f, chunk):
        c1 = min(c0 + chunk, f)
        a = jnp.dot(h, w13_ref[:, c0:c1], preferred_element_type=F32)
        g = jnp.dot(h, w13_ref[:, f + c0:f + c1], preferred_element_type=F32)
        part = jnp.dot((_silu(a) * g).astype(BF16), w2_ref[c0:c1, :], preferred_element_type=F32)
        acc = part if acc is None else acc + part
    o_ref[0] = x + mod[5:6] * acc


def _dense_ffn(x, mod, g, w13, w2, tm, chunk):
    bsz, s, d = x.shape
    f = w2.shape[0]
    const = lambda b, i: (0, 0)
    return pl.pallas_call(
        functools.partial(_ffn_kernel, chunk=chunk),
        grid=(bsz, s // tm),
        in_specs=[pl.BlockSpec((1, tm, d), lambda b, i: (b, i, 0)),
                  pl.BlockSpec((1, 6, d), lambda b, i: (b, 0, 0)),
                  pl.BlockSpec((1, d), const),
                  pl.BlockSpec((d, 2 * f), const, pipeline_mode=pl.Buffered(1)),
                  pl.BlockSpec((f, d), const, pipeline_mode=pl.Buffered(1))],
        out_specs=pl.BlockSpec((1, tm, d), lambda b, i: (b, i, 0)),
        out_shape=jax.ShapeDtypeStruct((bsz, s, d), F32),
        compiler_params=_cparams(("arbitrary", "arbitrary")),
        name="dense_ffn",
    )(x, mod, g.reshape(1, d), w13, w2)


def _rotary_heads(y, cos, sin_signed, low_half):
    outs = []
    for hh in range(y.shape[1] // HEAD_DIM):
        ys = y[:, hh * HEAD_DIM:(hh + 1) * HEAD_DIM]
        partner = jnp.where(low_half, pltpu.roll(ys, HEAD_DIM - ROT_DIM // 2, 1),
                            pltpu.roll(ys, ROT_DIM // 2, 1))
        outs.append(ys * cos + partner * sin_signed)
    return outs


def _kvq_kernel(x_ref, modkv_ref, mod_ref, gkv_ref, gq_ref, wkv_ref, wq_ref, pos_ref, freq_ref,
                q_ref, k_ref, vt_ref, km_ref):
    tm, d = x_ref.shape[1], x_ref.shape[2]
    modkv = modkv_ref[0]
    mod = mod_ref[0]
    lane = lax.broadcasted_iota(jnp.int32, (MOBA_BLOCK, HEAD_DIM), 1)
    low_half = lane < ROT_DIM // 2

    for r in range(tm // MOBA_BLOCK):
        rows = slice(r * MOBA_BLOCK, (r + 1) * MOBA_BLOCK)
        x = x_ref[0, rows, :]
        xn = x * lax.rsqrt(jnp.mean(x * x, axis=-1, keepdims=True) + NORM_EPS)
        hkv = ((xn * gkv_ref[...]) * (1.0 + modkv[1:2]) + modkv[0:1]).astype(BF16)
        hq = ((xn * gq_ref[...]) * (1.0 + mod[1:2]) + mod[0:1]).astype(BF16)
        kv = jnp.dot(hkv, wkv_ref[...], preferred_element_type=F32)
        qf = jnp.dot(hq, wq_ref[...], preferred_element_type=F32)

        n_grp = LANES // ROT_DIM
        grp = MOBA_BLOCK // n_grp
        lane_grp = lax.broadcasted_iota(jnp.int32, (grp, LANES), 1) // ROT_DIM
        packed = None
        for g in range(n_grp):
            first = r * MOBA_BLOCK + g * grp
            pos_g = pos_ref[0, first:first + grp, :].astype(F32)
            packed = pos_g if packed is None else jnp.where(lane_grp == g, pos_g, packed)
        ang = packed * freq_ref[...]
        cos_p = jnp.cos(ang)
        sin_p = jnp.sin(ang)
        rot = lax.broadcasted_iota(jnp.int32, (grp, LANES), 1) < ROT_DIM
        cos_parts, sin_parts = [], []
        for g in range(n_grp):
            shift = (LANES - g * ROT_DIM) % LANES
            cos_g = cos_p if g == 0 else pltpu.roll(cos_p, shift, 1)
            sin_g = sin_p if g == 0 else pltpu.roll(sin_p, shift, 1)
            cos_parts.append(jnp.where(rot, cos_g, 1.0))
            sin_parts.append(jnp.where(rot, sin_g, 0.0))
        cos = jnp.concatenate(cos_parts, axis=0)
        sin = jnp.concatenate(sin_parts, axis=0)
        sin_signed = jnp.where(low_half, -sin, sin)

        vt_ref[0, :, rows] = kv[:, d:].T.astype(BF16)
        k_heads = _rotary_heads(kv[:, :d], cos, sin_signed, low_half)
        q_heads = _rotary_heads(qf, cos, sin_signed, low_half)
        for hh in range(N_HEADS):
            hs = slice(hh * HEAD_DIM, (hh + 1) * HEAD_DIM)
            q_ref[0, rows, hs] = q_heads[hh].astype(BF16)
            k_ref[0, rows, hs] = k_heads[hh].astype(BF16)
            km_ref[0, r, :, hs] = jnp.mean(k_heads[hh], axis=0, keepdims=True)


def _kvq(x, modkv, mod, gkv, gq, wkv, wq, pos, freq, tm):
    bsz, s, d = x.shape
    nb = s // MOBA_BLOCK
    row = lambda b, j: (0, 0)
    act = jax.ShapeDtypeStruct((bsz, s, d), BF16)
    return pl.pallas_call(
        _kvq_kernel,
        grid=(bsz, s // tm),
        in_specs=[pl.BlockSpec((1, tm, d), lambda b, j: (b, j, 0)),
                  pl.BlockSpec((1, 2, d), lambda b, j: (b, 0, 0)),
                  pl.BlockSpec((1, 6, d), lambda b, j: (b, 0, 0)),
                  pl.BlockSpec((1, d), row),
                  pl.BlockSpec((1, d), row),
                  pl.BlockSpec((d, 2 * d), row),
                  pl.BlockSpec((d, d), row),
                  pl.BlockSpec((1, tm, 1), lambda b, j: (b, j, 0)),
                  pl.BlockSpec((1, HEAD_DIM), row)],
        out_specs=[pl.BlockSpec((1, tm, d), lambda b, j: (b, j, 0)),
                   pl.BlockSpec((1, tm, d), lambda b, j: (b, j, 0)),
                   pl.BlockSpec((1, d, tm), lambda b, j: (b, 0, j)),
                   pl.BlockSpec((1, tm // MOBA_BLOCK, 1, d), lambda b, j: (b, j, 0, 0))],
        out_shape=[act, act, jax.ShapeDtypeStruct((bsz, d, s), BF16),
                   jax.ShapeDtypeStruct((bsz, nb, 1, d), F32)],
        compiler_params=_cparams(("arbitrary", "arbitrary")),
        name="kvq_proj",
    )(x, modkv, mod, gkv.reshape(1, d), gq.reshape(1, d), wkv, wq, pos, freq)


def _attn_scores(own, nb, q, km, k_ref, s_scr, topk):
    contract_last = (((1,), (1,)), ((), ()))
    tq = q.shape[0]

    k_own = k_ref[0, own * MOBA_BLOCK:(own + 1) * MOBA_BLOCK, :]
    ext = lax.dot_general(jnp.concatenate([k_own, km], axis=0), q, contract_last,
                          preferred_element_type=F32)
    s = ext[:MOBA_BLOCK]
    causal = lax.broadcasted_iota(jnp.int32, s.shape, 0) <= lax.broadcasted_iota(jnp.int32, s.shape, 1)
    s = jnp.where(causal, s, -jnp.inf)
    s_scr[own] = s
    m = jnp.max(s, axis=0, keepdims=True)

    sel = None
    if own > topk:
        blk = lax.broadcasted_iota(jnp.int32, (nb, tq), 0)
        past = blk < own
        gate = jnp.where(past, ext[MOBA_BLOCK:MOBA_BLOCK + nb], -jnp.inf)
        rank = jnp.zeros((nb, tq), jnp.int32)
        for j in range(own):
            gj = gate[j:j + 1, :]
            rank = rank + jnp.where(gj > gate, 1, jnp.where((gj == gate) & (blk > j), 1, 0))
        sel = jnp.where(past & (rank < topk), 1.0, 0.0)

    for n in range(own):
        kn = k_ref[0, n * MOBA_BLOCK:(n + 1) * MOBA_BLOCK, :]
        s = lax.dot_general(kn, q, contract_last, preferred_element_type=F32)
        if sel is not None:
            s = jnp.where(sel[n:n + 1, :] > 0.0, s, -jnp.inf)
        s_scr[n] = s
        m = jnp.maximum(m, jnp.max(s, axis=0, keepdims=True))
    return m


def _attn_output(own, m, vt_ref, o_ref, s_scr, p_scr):
    c = HEAD_DIM ** -0.5 * LOG2_E
    l = None
    for n in range(own + 1):
        p = jnp.exp2((s_scr[n] - m) * c)
        bl = jnp.sum(p, axis=0, keepdims=True)
        l = bl if l is None else l + bl
        p_scr[n * MOBA_BLOCK:(n + 1) * MOBA_BLOCK, :] = p.astype(BF16)
    span = (own + 1) * MOBA_BLOCK
    acc = jnp.dot(vt_ref[0, :, :span], p_scr[:span, :], preferred_element_type=F32)
    o_ref[0] = (acc / l).T.astype(o_ref.dtype)


def _attn_kernel(q_ref, k_ref, vt_ref, km_ref, o_ref, s_scr, p_scr, *, topk):
    nb = q_ref.shape[1] // MOBA_BLOCK
    km = km_ref[0, 0]

    def rows(own):
        return slice(own * MOBA_BLOCK, (own + 1) * MOBA_BLOCK)

    def blocks(own):
        first = own * (own + 1) // 2
        return slice(first, first + own + 1), slice(first * MOBA_BLOCK, (first + own + 1) * MOBA_BLOCK)

    def scores(own):
        return _attn_scores(own, nb, q_ref[0, rows(own), :], km, k_ref, s_scr.at[blocks(own)[0]], topk)

    m = scores(0)
    for own in range(nb):
        m_next = scores(own + 1) if own + 1 < nb else None
        _attn_output(own, m, vt_ref, o_ref.at[:, rows(own), :], s_scr.at[blocks(own)[0]], p_scr.at[blocks(own)[1]])
        m = m_next


def _attention(q, k, vt, kmean, topk):
    bsz, s, d = q.shape
    nb = s // MOBA_BLOCK
    n_pairs = nb * (nb + 1) // 2
    return pl.pallas_call(
        functools.partial(_attn_kernel, topk=topk),
        grid=(bsz, N_HEADS),
        in_specs=[pl.BlockSpec((1, s, HEAD_DIM), lambda b, h: (b, 0, h)),
                  pl.BlockSpec((1, s, HEAD_DIM), lambda b, h: (b, 0, h)),
                  pl.BlockSpec((1, HEAD_DIM, s), lambda b, h: (b, h, 0)),
                  pl.BlockSpec((1, 1, kmean.shape[2], HEAD_DIM), lambda b, h: (b, h, 0, 0))],
        out_specs=pl.BlockSpec((1, s, HEAD_DIM), lambda b, h: (b, 0, h)),
        out_shape=jax.ShapeDtypeStruct((bsz, s, d), BF16),
        scratch_shapes=[pltpu.VMEM((n_pairs, MOBA_BLOCK, MOBA_BLOCK), F32),
                        pltpu.VMEM((n_pairs * MOBA_BLOCK, MOBA_BLOCK), BF16)],
        compiler_params=_cparams(("arbitrary", "arbitrary")),
        name="moba_attention",
    )(q, k, vt, kmean)


def _oproj_router_kernel(o_ref, x_ref, mod_ref, g_ref, wo_ref, rw_ref, rb_ref, x_out, route_out, ids_out):
    mod = mod_ref[0]
    mix = jnp.dot(o_ref[0], wo_ref[...], preferred_element_type=F32)
    x = x_ref[0] + mod[2:3] * mix
    x_out[0] = x
    h = _rms_mod(x, g_ref[...], mod[3:4], mod[4:5])

    logits = jnp.dot(h.astype(BF16), rw_ref[...], preferred_element_type=F32) + rb_ref[...]
    lane = lax.broadcasted_iota(jnp.int32, logits.shape, 1)
    logits = jnp.where(lane < N_EXPERTS, logits, -jnp.inf)
    v1 = jnp.max(logits, axis=-1, keepdims=True)
    i1 = jnp.min(jnp.where(logits == v1, lane, LANES), axis=-1, keepdims=True)
    rest = jnp.where(lane == i1, -jnp.inf, logits)
    v2 = jnp.max(rest, axis=-1, keepdims=True)
    i2 = jnp.min(jnp.where(rest == v2, lane, LANES), axis=-1, keepdims=True)
    e2 = jnp.exp(v2 - v1)
    den = 1.0 + e2
    route = jnp.where(lane == 0, i1.astype(F32),
                      jnp.where(lane == 1, i2.astype(F32),
                                jnp.where(lane == 2, 1.0 / den,
                                          jnp.where(lane == 3, e2 / den, 0.0))))
    route_out[0] = route
    ids_out[0] = jnp.transpose(route[:, 0:SUBLANES])


def _oproj_router(o, x, mod, g, wo, rw, rb, tm):
    bsz, s, d = x.shape
    nt = s // tm
    row = lambda b, j: (0, 0)
    tile = lambda b, j: (b, j, 0)
    return pl.pallas_call(
        _oproj_router_kernel,
        grid=(bsz, s // tm),
        in_specs=[pl.BlockSpec((1, tm, d), tile),
                  pl.BlockSpec((1, tm, d), tile),
                  pl.BlockSpec((1, 6, d), lambda b, j: (b, 0, 0)),
                  pl.BlockSpec((1, d), row),
                  pl.BlockSpec((d, d), row),
                  pl.BlockSpec((d, LANES), row),
                  pl.BlockSpec((1, LANES), row)],
        out_specs=[pl.BlockSpec((1, tm, d), tile),
                   pl.BlockSpec((1, tm, LANES), tile),
                   pl.BlockSpec((1, SUBLANES, tm), lambda b, j: (b * nt + j, 0, 0))],
        out_shape=[jax.ShapeDtypeStruct((bsz, s, d), F32),
                   jax.ShapeDtypeStruct((bsz, s, LANES), F32),
                   jax.ShapeDtypeStruct((bsz * nt, SUBLANES, tm), F32)],
        compiler_params=_cparams(("arbitrary", "arbitrary")),
        name="oproj_router",
    )(o, x, mod, g.reshape(1, d), wo, rw, rb)


DMA_UNROLL = 8


def _pack_bf16_pairs(h):
    half = h.shape[1] // 2
    hb = h.astype(BF16).astype(F32)
    lo = lax.shift_right_logical(pltpu.bitcast(hb[:, :half], jnp.uint32), jnp.uint32(16))
    hi = pltpu.bitcast(hb[:, half:], jnp.uint32) & jnp.uint32(0xFFFF0000)
    return lo | hi


def _unpack_bf16_pairs(w):
    lo = pltpu.bitcast(lax.shift_left(w, jnp.uint32(16)), F32)
    hi = pltpu.bitcast(w & jnp.uint32(0xFFFF0000), F32)
    return lo.astype(BF16), hi.astype(BF16)


def _dispatch_copy(slot_ref, hbuf, hs_hbm, sem, r, k):
    return pltpu.make_async_copy(hbuf.at[pl.ds(r, 1)], hs_hbm.at[pl.ds(slot_ref[0, 0, TOP_K * r + k], 1)], sem)


def _pad_copy(pad_ref, zbuf, hs_hbm, sem, e):
    return pltpu.make_async_copy(zbuf, hs_hbm.at[pl.ds(pl.multiple_of(pad_ref[e], SUBLANES), zbuf.shape[0])], sem)


def _dispatch_kernel(slot_ref, pad_ref, x_ref, mod_ref, g_ref, hs_out, hbuf, zbuf, sem, pad_sem):
    tm = x_ref.shape[1]

    @pl.when((pl.program_id(0) == 0) & (pl.program_id(1) == 0))
    def _():
        zbuf[...] = jnp.zeros(zbuf.shape, zbuf.dtype)
        for e in range(N_EXPERTS):
            _pad_copy(pad_ref, zbuf, hs_out, pad_sem, e).start()
        for e in range(N_EXPERTS):
            _pad_copy(pad_ref, zbuf, hs_out, pad_sem, e).wait()

        def clear_tile(t, c):
            tile = pltpu.make_async_copy(
                zbuf, hs_out.at[pl.ds(pl.multiple_of(t * zbuf.shape[0], zbuf.shape[0]), zbuf.shape[0])], pad_sem)
            tile.start()
            tile.wait()
            return c

        lax.fori_loop(pad_ref[N_EXPERTS], hs_out.shape[0] // zbuf.shape[0], clear_tile, 0)

    mod = mod_ref[0]
    hbuf[...] = _pack_bf16_pairs(_rms_mod(x_ref[0], g_ref[...], mod[3:4], mod[4:5]))

    def start(r, c):
        for k in range(TOP_K):
            _dispatch_copy(slot_ref, hbuf, hs_out, sem, r, k).start()
        return c

    def wait(r, c):
        for k in range(TOP_K):
            _dispatch_copy(slot_ref, hbuf, hs_out, sem, r, k).wait()
        return c

    lax.fori_loop(0, tm, start, 0, unroll=DMA_UNROLL)
    lax.fori_loop(0, tm, wait, 0, unroll=DMA_UNROLL)


def _dispatch(slots, pad_start, x, mod, g, n_slots, tm, pad_rows):
    bsz, s, d = x.shape
    nt = s // tm
    return pl.pallas_call(
        _dispatch_kernel,
        grid=(bsz, nt),
        in_specs=[pl.BlockSpec((1, 1, TOP_K * tm), lambda b, j: (b * nt + j, 0, 0), memory_space=pltpu.SMEM),
                  pl.BlockSpec(memory_space=pltpu.SMEM),
                  pl.BlockSpec((1, tm, d), lambda b, j: (b, j, 0)),
                  pl.BlockSpec((1, 6, d), lambda b, j: (b, 0, 0)),
                  pl.BlockSpec((1, d), lambda b, j: (0, 0))],
        out_specs=pl.BlockSpec(memory_space=pl.ANY),
        out_shape=jax.ShapeDtypeStruct((n_slots, d // 2), jnp.uint32),
        scratch_shapes=[pltpu.VMEM((tm, d // 2), jnp.uint32), pltpu.VMEM((pad_rows, d // 2), jnp.uint32),
                        pltpu.SemaphoreType.DMA(()), pltpu.SemaphoreType.DMA(())],
        compiler_params=_cparams(("arbitrary", "arbitrary")),
        name="moe_dispatch",
    )(slots.reshape(bsz * nt, 1, TOP_K * tm), pad_start, x, mod, g.reshape(1, d))


def _moe_kernel(te_ref, nv_ref, hs_ref, wa_ref, wg_ref, w2_ref, o_ref, h_scr, acc_scr):
    del te_ref
    i = pl.program_id(0)
    j = pl.program_id(1)
    valid = i < nv_ref[0]

    @pl.when(valid & (j == 0))
    def _():
        half = hs_ref.shape[1]
        lo, hi = _unpack_bf16_pairs(hs_ref[...])
        h_scr[:, :half] = lo
        h_scr[:, half:] = hi
        acc_scr[...] = jnp.zeros_like(acc_scr)

    @pl.when(valid)
    def _():
        h = h_scr[...]
        a = jnp.dot(h, wa_ref[0], preferred_element_type=F32)
        g = jnp.dot(h, wg_ref[0], preferred_element_type=F32)
        acc_scr[...] += jnp.dot((_silu(a) * g).astype(BF16), w2_ref[0], preferred_element_type=F32)

    last = j == pl.num_programs(1) - 1

    @pl.when(valid & last)
    def _():
        o_ref[...] = acc_scr[...]

    @pl.when(jnp.logical_not(valid) & last)
    def _():
        o_ref[...] = jnp.zeros_like(o_ref)


def _moe_ffn(tile_expert, n_valid, hs, w13, w2, tm, tf):
    n_slots, half = hs.shape
    d = 2 * half
    f = w2.shape[1]
    nf = f // tf
    nt = n_slots // tm

    def row_idx(i, j, te, nv):
        return (jnp.minimum(i, nv[0] - 1), 0)

    def ff(i, j, nv):
        return jnp.where(i < nv[0], j, nf - 1)

    grid_spec = pltpu.PrefetchScalarGridSpec(
        num_scalar_prefetch=2,
        grid=(nt, nf),
        in_specs=[pl.BlockSpec((tm, half), row_idx),
                  pl.BlockSpec((1, d, tf), lambda i, j, te, nv: (te[i], 0, ff(i, j, nv))),
                  pl.BlockSpec((1, d, tf), lambda i, j, te, nv: (te[i], 0, nf + ff(i, j, nv))),
                  pl.BlockSpec((1, tf, d), lambda i, j, te, nv: (te[i], ff(i, j, nv), 0))],
        out_specs=pl.BlockSpec((tm, d), lambda i, j, te, nv: (i, 0)),
        scratch_shapes=[pltpu.VMEM((tm, d), BF16), pltpu.VMEM((tm, d), F32)],
    )
    return pl.pallas_call(
        _moe_kernel,
        grid_spec=grid_spec,
        out_shape=jax.ShapeDtypeStruct((n_slots, d), F32),
        compiler_params=_cparams(("arbitrary", "arbitrary")),
        name="moe_ffn",
    )(tile_expert, n_valid, hs, w13, w13, w2)


def _combine_copy(slot_ref, ys_hbm, buf, sems, par, r, k):
    return pltpu.make_async_copy(ys_hbm.at[pl.ds(slot_ref[0, 0, TOP_K * r + k], 1)],
                                 buf.at[par, k, pl.ds(r, 1)], sems.at[par])


def _combine_kernel(slot_ref, next_slot_ref, x_ref, route_ref, mod_ref, g_ref, ys_hbm, o_ref, buf, sems):
    tc = x_ref.shape[1]
    g = pl.program_id(0) * pl.num_programs(1) + pl.program_id(1)
    n_steps = pl.num_programs(0) * pl.num_programs(1)
    par = lax.rem(g, 2)

    def fetch(slots, parity):
        def start(r, c):
            for k in range(TOP_K):
                _combine_copy(slots, ys_hbm, buf, sems, parity, r, k).start()
            return c
        lax.fori_loop(0, tc, start, 0, unroll=DMA_UNROLL)

    @pl.when(g == 0)
    def _():
        fetch(slot_ref, par)

    @pl.when(g + 1 < n_steps)
    def _():
        fetch(next_slot_ref, 1 - par)

    def wait(r, c):
        for k in range(TOP_K):
            _combine_copy(slot_ref, ys_hbm, buf, sems, par, r, k).wait()
        return c

    lax.fori_loop(0, tc, wait, 0, unroll=DMA_UNROLL)

    route = route_ref[0]
    y = route[:, 2:3] * buf[par, 0] + route[:, 3:4] * buf[par, 1]
    x = x_ref[0] + mod_ref[0][5:6] * y
    o_ref[0] = (x * lax.rsqrt(jnp.mean(x * x, axis=-1, keepdims=True) + NORM_EPS)) * g_ref[...]


def _combine(slots, x, route, mod, g, ys, tc):
    bsz, s, d = x.shape
    nt = s // tc
    tile = lambda b, j: (b, j, 0)
    slots = slots.reshape(bsz * nt, 1, TOP_K * tc)
    return pl.pallas_call(
        _combine_kernel,
        grid=(bsz, nt),
        in_specs=[pl.BlockSpec((1, 1, TOP_K * tc), lambda b, j: (b * nt + j, 0, 0), memory_space=pltpu.SMEM),
                  pl.BlockSpec((1, 1, TOP_K * tc), lambda b, j: (jnp.minimum(b * nt + j + 1, bsz * nt - 1), 0, 0),
                               memory_space=pltpu.SMEM),
                  pl.BlockSpec((1, tc, d), tile),
                  pl.BlockSpec((1, tc, LANES), tile),
                  pl.BlockSpec((1, 6, d), lambda b, j: (b, 0, 0)),
                  pl.BlockSpec((1, d), lambda b, j: (0, 0)),
                  pl.BlockSpec(memory_space=pl.ANY)],
        out_specs=pl.BlockSpec((1, tc, d), tile),
        out_shape=jax.ShapeDtypeStruct((bsz, s, d), F32),
        scratch_shapes=[pltpu.VMEM((2, TOP_K, tc, d), F32), pltpu.SemaphoreType.DMA((2,))],
        compiler_params=_cparams(("arbitrary", "arbitrary")),
        name="moe_combine",
    )(slots, slots, x, route, mod, g.reshape(1, d), ys)


def _routing_plan(ids, tm):
    t = ids.shape[0] * ids.shape[2]
    flat_e = ids[:, :TOP_K, :].astype(jnp.int32).transpose(0, 2, 1).reshape(-1)
    onehot = (flat_e[:, None] == jnp.arange(N_EXPERTS, dtype=jnp.int32)[None, :]).astype(jnp.int32)
    csum = jnp.cumsum(onehot, axis=0)
    rank = jnp.sum((csum - onehot) * onehot, axis=1)
    counts = csum[-1]
    padded = ((counts + tm - 1) // tm) * tm
    ends = jnp.cumsum(padded)
    starts = ends - padded
    slots = jnp.sum(starts[None, :] * onehot, axis=1) + rank
    n_tiles = TOP_K * t // tm + N_EXPERTS
    tile_start = jnp.arange(n_tiles, dtype=jnp.int32) * tm
    n_valid = (ends[-1] // tm).astype(jnp.int32)
    tile_expert = jnp.sum((tile_start[:, None] >= ends[None, :]).astype(jnp.int32), axis=1)
    last_expert = jnp.sum((ends[-1] - 1 >= ends).astype(jnp.int32))
    tile_expert = jnp.where(tile_start < ends[-1], tile_expert, last_expert).astype(jnp.int32)
    pad_start = ((starts + counts) // SUBLANES * SUBLANES).astype(jnp.int32)
    pad_start = jnp.concatenate([pad_start, n_valid.reshape(1)])
    return slots.astype(jnp.int32), tile_expert, n_valid.reshape(1), n_tiles, pad_start


def kernel(x, c, positions, ada_w, ada_b, norm1_g, norm2_g, conv_w1, conv_b1, conv_dw_w, conv_dw_b,
           conv_ln_g, conv_ln_b, conv_w2, conv_b2, kv_ada_w, kv_ada_b, kv_norm_g, w_kv, w_q, w_o,
           ffn_w13, ffn_w2, router_w, router_b, moe_w13, moe_w2, final_g):
    bsz, s, d = x.shape
    assert ada_w.shape[0] == 2 and d == N_HEADS * HEAD_DIM and s % MOBA_BLOCK == 0
    nb = s // MOBA_BLOCK
    topk = min(MOBA_TOPK, max(nb - 1, 1))
    tm = 512
    moe_tm = 1024

    mod0 = _ada(c, ada_w[0], ada_b[0]).reshape(bsz, 6, d)
    mod1 = _ada(c, ada_w[1], ada_b[1]).reshape(bsz, 6, d)
    modkv = _ada(c, kv_ada_w, kv_ada_b).reshape(bsz, 2, d)

    x = _conv_mixer(x, mod0, norm1_g[0], conv_w1[0].astype(BF16), conv_b1[0], conv_dw_w[0], conv_dw_b[0],
                    conv_ln_g[0], conv_ln_b[0], conv_w2[0].astype(BF16), conv_b2[0], tm)
    x = _dense_ffn(x, mod0, norm2_g[0], ffn_w13[0].astype(BF16), ffn_w2[0].astype(BF16), tm, 512)

    inv_freq = ROPE_THETA ** (-jnp.arange(0, ROT_DIM, 2, dtype=F32) / ROT_DIM)
    freq = jnp.tile(inv_freq, HEAD_DIM // (ROT_DIM // 2)).reshape(1, HEAD_DIM)
    q, k, vt, kmean = _kvq(x, modkv, mod1, kv_norm_g, norm1_g[1], w_kv.astype(BF16), w_q[0].astype(BF16),
                          positions.reshape(bsz, s, 1), freq, tm)
    kmean = kmean.reshape(bsz, nb, N_HEADS, HEAD_DIM).transpose(0, 2, 1, 3).astype(BF16)
    kmean = jnp.pad(kmean, ((0, 0), (0, 0), (0, -nb % BF16_ROWS), (0, 0)))

    o = _attention(q, k, vt, kmean, topk)
    rw = jnp.zeros((d, LANES), BF16).at[:, :N_EXPERTS].set(router_w[0].astype(BF16))
    rb = jnp.zeros((1, LANES), F32).at[0, :N_EXPERTS].set(router_b[0])
    x, route, ids = _oproj_router(o, x, mod1, norm2_g[1], w_o[0].astype(BF16), rw, rb, tm)

    slots, tile_expert, n_valid, n_tiles, pad_start = _routing_plan(ids, moe_tm)
    hs = _dispatch(slots, pad_start, x, mod1, norm2_g[1], n_tiles * moe_tm, tm, moe_tm)
    ys = _moe_ffn(tile_expert, n_valid, hs, moe_w13[0].astype(BF16), moe_w2[0].astype(BF16), moe_tm, 512)
    return _combine(slots, x, route, mod1, final_g, ys, tm)
```

```python
import functools

import jax
import jax.numpy as jnp
from jax import lax
from jax.experimental import pallas as pl
from jax.experimental.pallas import tpu as pltpu

N_HEADS = 8
HEAD_DIM = 128
ROT_DIM = HEAD_DIM // 4
ROPE_THETA = 500000.0
MOBA_BLOCK = 256
MOBA_TOPK = 3
CONV_WIDTH = 31
N_EXPERTS = 8
TOP_K = 2
NORM_EPS = 1e-6
LOG2_E = 1.4426950408889634

LANES = 128
SUBLANES = 8
BF16_ROWS = 16
CONV_HALO = 32
VMEM_LIMIT = 56 * 1024 * 1024

ROW_TILE = 512
FFN_CHUNK = 512
MOE_ROW_TILE = 1024
MOE_FF_TILE = 512

F32 = jnp.float32
BF16 = jnp.bfloat16
HIGHEST = lax.Precision.HIGHEST


def _cparams(sem):
    return pltpu.CompilerParams(dimension_semantics=sem, vmem_limit_bytes=VMEM_LIMIT)


def _rms_mod(x, g, shift, scale):
    y = x * lax.rsqrt(jnp.mean(x * x, axis=-1, keepdims=True) + NORM_EPS)
    return (y * g) * (1.0 + scale) + shift


def _silu(a):
    return a * jax.nn.sigmoid(a)


def _ada_kernel(c_ref, w_ref, b_ref, o_ref):
    o_ref[...] = jnp.dot(_silu(c_ref[...]), w_ref[...], preferred_element_type=F32,
                         precision=HIGHEST) + b_ref[...]


def _ada(c, w, b):
    bsz, d = c.shape
    n = w.shape[1]
    tn = min(n, 2048)
    return pl.pallas_call(
        _ada_kernel,
        grid=(n // tn,),
        in_specs=[pl.BlockSpec((bsz, d), lambda j: (0, 0)),
                  pl.BlockSpec((d, tn), lambda j: (0, j)),
                  pl.BlockSpec((1, tn), lambda j: (0, j))],
        out_specs=pl.BlockSpec((bsz, tn), lambda j: (0, j)),
        out_shape=jax.ShapeDtypeStruct((bsz, n), F32),
        compiler_params=_cparams(("arbitrary",)),
        name="ada",
    )(c, w, b.reshape(1, n))


CONV_ROWS = 128


def _conv_mixer_kernel(x_ref, mod_ref, g_ref, w1_ref, b1_ref, dw_ref, dwb_ref, lng_ref, lnb_ref,
                       w2_ref, b2_ref, o_ref, ubuf, zbuf):
    tm, d = x_ref.shape[1], x_ref.shape[2]
    j = pl.program_id(1)

    @pl.when(j == 0)
    def _():
        ubuf[0:CONV_HALO, :] = jnp.zeros((CONV_HALO, d), F32)

    @pl.when(j > 0)
    def _():
        ubuf[0:CONV_HALO, :] = ubuf[tm:tm + CONV_HALO, :]

    x = x_ref[0]
    mod = mod_ref[0]
    h = _rms_mod(x, g_ref[...], mod[0:1], mod[1:2]).astype(BF16)
    ag = jnp.dot(h, w1_ref[...], preferred_element_type=F32) + b1_ref[...]
    ubuf[CONV_HALO:CONV_HALO + tm, :] = ag[:, :d] * jax.nn.sigmoid(ag[:, d:])

    lead = CONV_HALO - (CONV_WIDTH - 1)
    win_rows = CONV_ROWS + CONV_HALO

    def chunk(r, carry):
        base = pl.multiple_of(r * CONV_ROWS, CONV_ROWS)
        for lc in range(d // LANES):
            lanes = slice(lc * LANES, (lc + 1) * LANES)
            win = ubuf[pl.ds(base, win_rows), lanes]
            acc = jnp.broadcast_to(dwb_ref[:, lanes], (CONV_ROWS, LANES))
            for phase in range(SUBLANES):
                taps = [k for k in range(CONV_WIDTH) if (lead + k) % SUBLANES == phase]
                if not taps:
                    continue
                shifted = win if phase == 0 else pltpu.roll(win, win_rows - phase, 0)
                for k in taps:
                    off = lead + k - phase
                    acc = acc + dw_ref[k:k + 1, lanes] * shifted[off:off + CONV_ROWS, :]
            zbuf[pl.ds(base, CONV_ROWS), lanes] = acc
        return carry

    lax.fori_loop(0, tm // CONV_ROWS, chunk, 0)

    z = zbuf[...]
    mu = jnp.mean(z, axis=-1, keepdims=True)
    zc = z - mu
    var = jnp.mean(zc * zc, axis=-1, keepdims=True)
    zn = (zc * lax.rsqrt(var + NORM_EPS)) * lng_ref[...] + lnb_ref[...]
    mix = jnp.dot(_silu(zn).astype(BF16), w2_ref[...], preferred_element_type=F32) + b2_ref[...]
    o_ref[0] = x + mod[2:3] * mix


def _conv_mixer(x, mod, g, w1, b1, dw, dwb, lng, lnb, w2, b2, tm):
    bsz, s, d = x.shape
    row = lambda b, j: (0, 0)
    return pl.pallas_call(
        _conv_mixer_kernel,
        grid=(bsz, s // tm),
        in_specs=[pl.BlockSpec((1, tm, d), lambda b, j: (b, j, 0)),
                  pl.BlockSpec((1, 6, d), lambda b, j: (b, 0, 0)),
                  pl.BlockSpec((1, d), row),
                  pl.BlockSpec((d, 2 * d), row),
                  pl.BlockSpec((1, 2 * d), row),
                  pl.BlockSpec((CONV_WIDTH, d), row),
                  pl.BlockSpec((1, d), row),
                  pl.BlockSpec((1, d), row),
                  pl.BlockSpec((1, d), row),
                  pl.BlockSpec((d, d), row),
                  pl.BlockSpec((1, d), row)],
        out_specs=pl.BlockSpec((1, tm, d), lambda b, j: (b, j, 0)),
        out_shape=jax.ShapeDtypeStruct((bsz, s, d), F32),
        scratch_shapes=[pltpu.VMEM((tm + CONV_HALO, d), F32), pltpu.VMEM((tm, d), F32)],
        compiler_params=_cparams(("arbitrary", "arbitrary")),
        name="conv_mixer",
    )(x, mod, g.reshape(1, d), w1, b1.reshape(1, 2 * d), dw, dwb.reshape(1, d), lng.reshape(1, d),
      lnb.reshape(1, d), w2, b2.reshape(1, d))


def _ffn_kernel(x_ref, mod_ref, g_ref, w13_ref, w2_ref, o_ref, *, chunk):
    f = w2_ref.shape[0]
    x = x_ref[0]
    mod = mod_ref[0]
    h = _rms_mod(x, g_ref[...], mod[3:4], mod[4:5]).astype(BF16)
    acc = None
    for c0 in range(0, f, chunk):
        c1 = min(c0 + chunk, f)
        a = jnp.dot(h, w13_ref[:, c0:c1], preferred_element_type=F32)
        g = jnp.dot(h, w13_ref[:, f + c0:f + c1], preferred_element_type=F32)
        part = jnp.dot((_silu(a) * g).astype(BF16), w2_ref[c0:c1, :], preferred_element_type=F32)
        acc = part if acc is None else acc + part
    o_ref[0] = x + mod[5:6] * acc


def _dense_ffn(x, mod, g, w13, w2, tm, chunk):
    bsz, s, d = x.shape
    f = w2.shape[0]
    const = lambda b, i: (0, 0)
    return pl.pallas_call(
        functools.partial(_ffn_kernel, chunk=chunk),
        grid=(bsz, s // tm),
        in_specs=[pl.BlockSpec((1, tm, d), lambda b, i: (b, i, 0)),
                  pl.BlockSpec((1, 6, d), lambda b, i: (b, 0, 0)),
                  pl.BlockSpec((1, d), const),
                  pl.BlockSpec((d, 2 * f), const, pipeline_mode=pl.Buffered(1)),
                  pl.BlockSpec((f, d), const, pipeline_mode=pl.Buffered(1))],
        out_specs=pl.BlockSpec((1, tm, d), lambda b, i: (b, i, 0)),
        out_shape=jax.ShapeDtypeStruct((bsz, s, d), F32),
        compiler_params=_cparams(("arbitrary", "arbitrary")),
        name="dense_ffn",
    )(x, mod, g.reshape(1, d), w13, w2)


def _rotary_heads(y, cos, sin_signed, low_half):
    outs = []
    for hh in range(y.shape[1] // HEAD_DIM):
        ys = y[:, hh * HEAD_DIM:(hh + 1) * HEAD_DIM]
        partner = jnp.where(low_half, pltpu.roll(ys, HEAD_DIM - ROT_DIM // 2, 1),
                            pltpu.roll(ys, ROT_DIM // 2, 1))
        outs.append(ys * cos + partner * sin_signed)
    return outs


def _kvq_kernel(x_ref, modkv_ref, mod_ref, gkv_ref, gq_ref, wkv_ref, wq_ref, pos_ref, freq_ref,
                q_ref, k_ref, vt_ref, km_ref):
    tm, d = x_ref.shape[1], x_ref.shape[2]
    modkv = modkv_ref[0]
    mod = mod_ref[0]
    lane = lax.broadcasted_iota(jnp.int32, (MOBA_BLOCK, HEAD_DIM), 1)
    low_half = lane < ROT_DIM // 2

    for r in range(tm // MOBA_BLOCK):
        rows = slice(r * MOBA_BLOCK, (r + 1) * MOBA_BLOCK)
        x = x_ref[0, rows, :]
        xn = x * lax.rsqrt(jnp.mean(x * x, axis=-1, keepdims=True) + NORM_EPS)
        hkv = ((xn * gkv_ref[...]) * (1.0 + modkv[1:2]) + modkv[0:1]).astype(BF16)
        hq = ((xn * gq_ref[...]) * (1.0 + mod[1:2]) + mod[0:1]).astype(BF16)
        kv = jnp.dot(hkv, wkv_ref[...], preferred_element_type=F32)
        qf = jnp.dot(hq, wq_ref[...], preferred_element_type=F32)

        n_grp = LANES // ROT_DIM
        grp = MOBA_BLOCK // n_grp
        lane_grp = lax.broadcasted_iota(jnp.int32, (grp, LANES), 1) // ROT_DIM
        packed = None
        for g in range(n_grp):
            first = r * MOBA_BLOCK + g * grp
            pos_g = pos_ref[0, first:first + grp, :].astype(F32)
            packed = pos_g if packed is None else jnp.where(lane_grp == g, pos_g, packed)
        ang = packed * freq_ref[...]
        cos_p = jnp.cos(ang)
        sin_p = jnp.sin(ang)
        rot = lax.broadcasted_iota(jnp.int32, (grp, LANES), 1) < ROT_DIM
        cos_parts, sin_parts = [], []
        for g in range(n_grp):
            shift = (LANES - g * ROT_DIM) % LANES
            cos_g = cos_p if g == 0 else pltpu.roll(cos_p, shift, 1)
            sin_g = sin_p if g == 0 else pltpu.roll(sin_p, shift, 1)
            cos_parts.append(jnp.where(rot, cos_g, 1.0))
            sin_parts.append(jnp.where(rot, sin_g, 0.0))
        cos = jnp.concatenate(cos_parts, axis=0)
        sin = jnp.concatenate(sin_parts, axis=0)
        sin_signed = jnp.where(low_half, -sin, sin)

        vt_ref[0, :, rows] = kv[:, d:].T.astype(BF16)
        k_heads = _rotary_heads(kv[:, :d], cos, sin_signed, low_half)
        q_heads = _rotary_heads(qf, cos, sin_signed, low_half)
        for hh in range(N_HEADS):
            hs = slice(hh * HEAD_DIM, (hh + 1) * HEAD_DIM)
            q_ref[0, rows, hs] = q_heads[hh].astype(BF16)
            k_ref[0, rows, hs] = k_heads[hh].astype(BF16)
            km_ref[0, r, :, hs] = jnp.mean(k_heads[hh], axis=0, keepdims=True)


def _kvq(x, modkv, mod, gkv, gq, wkv, wq, pos, freq, tm):
    bsz, s, d = x.shape
    nb = s // MOBA_BLOCK
    row = lambda b, j: (0, 0)
    act = jax.ShapeDtypeStruct((bsz, s, d), BF16)
    return pl.pallas_call(
        _kvq_kernel,
        grid=(bsz, s // tm),
        in_specs=[pl.BlockSpec((1, tm, d), lambda b, j: (b, j, 0)),
                  pl.BlockSpec((1, 2, d), lambda b, j: (b, 0, 0)),
                  pl.BlockSpec((1, 6, d), lambda b, j: (b, 0, 0)),
                  pl.BlockSpec((1, d), row),
                  pl.BlockSpec((1, d), row),
                  pl.BlockSpec((d, 2 * d), row),
                  pl.BlockSpec((d, d), row),
                  pl.BlockSpec((1, tm, 1), lambda b, j: (b, j, 0)),
                  pl.BlockSpec((1, HEAD_DIM), row)],
        out_specs=[pl.BlockSpec((1, tm, d), lambda b, j: (b, j, 0)),
                   pl.BlockSpec((1, tm, d), lambda b, j: (b, j, 0)),
                   pl.BlockSpec((1, d, tm), lambda b, j: (b, 0, j)),
                   pl.BlockSpec((1, tm // MOBA_BLOCK, 1, d), lambda b, j: (b, j, 0, 0))],
        out_shape=[act, act, jax.ShapeDtypeStruct((bsz, d, s), BF16),
                   jax.ShapeDtypeStruct((bsz, nb, 1, d), F32)],
        compiler_params=_cparams(("arbitrary", "arbitrary")),
        name="kvq_proj",
    )(x, modkv, mod, gkv.reshape(1, d), gq.reshape(1, d), wkv, wq, pos, freq)


def _attn_scores(own, nb, q, km, k_ref, s_scr, topk):
    contract_last = (((1,), (1,)), ((), ()))
    tq = q.shape[0]

    k_own = k_ref[0, own * MOBA_BLOCK:(own + 1) * MOBA_BLOCK, :]
    ext = lax.dot_general(jnp.concatenate([k_own, km], axis=0), q, contract_last,
                          preferred_element_type=F32)
    s = ext[:MOBA_BLOCK]
    causal = lax.broadcasted_iota(jnp.int32, s.shape, 0) <= lax.broadcasted_iota(jnp.int32, s.shape, 1)
    s = jnp.where(causal, s, -jnp.inf)
    s_scr[own] = s
    m = jnp.max(s, axis=0, keepdims=True)

    sel = None
    if own > topk:
        blk = lax.broadcasted_iota(jnp.int32, (nb, tq), 0)
        past = blk < own
        gate = jnp.where(past, ext[MOBA_BLOCK:MOBA_BLOCK + nb], -jnp.inf)
        rank = jnp.zeros((nb, tq), jnp.int32)
        for j in range(own):
            gj = gate[j:j + 1, :]
            rank = rank + jnp.where(gj > gate, 1, jnp.where((gj == gate) & (blk > j), 1, 0))
        sel = jnp.where(past & (rank < topk), 1.0, 0.0)

    for n in range(own):
        kn = k_ref[0, n * MOBA_BLOCK:(n + 1) * MOBA_BLOCK, :]
        s = lax.dot_general(kn, q, contract_last, preferred_element_type=F32)
        if sel is not None:
            s = jnp.where(sel[n:n + 1, :] > 0.0, s, -jnp.inf)
        s_scr[n] = s
        m = jnp.maximum(m, jnp.max(s, axis=0, keepdims=True))
    return m


def _attn_output(own, m, vt_ref, o_ref, s_scr, p_scr):
    c = HEAD_DIM ** -0.5 * LOG2_E
    l = None
    for n in range(own + 1):
        p = jnp.exp2((s_scr[n] - m) * c)
        bl = jnp.sum(p, axis=0, keepdims=True)
        l = bl if l is None else l + bl
        p_scr[n * MOBA_BLOCK:(n + 1) * MOBA_BLOCK, :] = p.astype(BF16)
    span = (own + 1) * MOBA_BLOCK
    acc = jnp.dot(vt_ref[0, :, :span], p_scr[:span, :], preferred_element_type=F32)
    o_ref[0] = (acc / l).T.astype(o_ref.dtype)


def _attn_kernel(q_ref, k_ref, vt_ref, km_ref, o_ref, s_scr, p_scr, *, topk):
    nb = q_ref.shape[1] // MOBA_BLOCK
    km = km_ref[0, 0]

    def rows(own):
        return slice(own * MOBA_BLOCK, (own + 1) * MOBA_BLOCK)

    def blocks(own):
        first = own * (own + 1) // 2
        return slice(first, first + own + 1), slice(first * MOBA_BLOCK, (first + own + 1) * MOBA_BLOCK)

    def scores(own):
        return _attn_scores(own, nb, q_ref[0, rows(own), :], km, k_ref, s_scr.at[blocks(own)[0]], topk)

    m = scores(0)
    for own in range(nb):
        m_next = scores(own + 1) if own + 1 < nb else None
        _attn_output(own, m, vt_ref, o_ref.at[:, rows(own), :], s_scr.at[blocks(own)[0]], p_scr.at[blocks(own)[1]])
        m = m_next


def _attention(q, k, vt, kmean, topk):
    bsz, s, d = q.shape
    nb = s // MOBA_BLOCK
    n_pairs = nb * (nb + 1) // 2
    return pl.pallas_call(
        functools.partial(_attn_kernel, topk=topk),
        grid=(bsz, N_HEADS),
        in_specs=[pl.BlockSpec((1, s, HEAD_DIM), lambda b, h: (b, 0, h)),
                  pl.BlockSpec((1, s, HEAD_DIM), lambda b, h: (b, 0, h)),
                  pl.BlockSpec((1, HEAD_DIM, s), lambda b, h: (b, h, 0)),
                  pl.BlockSpec((1, 1, kmean.shape[2], HEAD_DIM), lambda b, h: (b, h, 0, 0))],
        out_specs=pl.BlockSpec((1, s, HEAD_DIM), lambda b, h: (b, 0, h)),
        out_shape=jax.ShapeDtypeStruct((bsz, s, d), BF16),
        scratch_shapes=[pltpu.VMEM((n_pairs, MOBA_BLOCK, MOBA_BLOCK), F32),
                        pltpu.VMEM((n_pairs * MOBA_BLOCK, MOBA_BLOCK), BF16)],
        compiler_params=_cparams(("arbitrary", "arbitrary")),
        name="moba_attention",
    )(q, k, vt, kmean)


def _oproj_router_kernel(o_ref, x_ref, mod_ref, g_ref, wo_ref, rw_ref, rb_ref, x_out, route_out, ids_out):
    mod = mod_ref[0]
    mix = jnp.dot(o_ref[0], wo_ref[...], preferred_element_type=F32)
    x = x_ref[0] + mod[2:3] * mix
    x_out[0] = x
    h = _rms_mod(x, g_ref[...], mod[3:4], mod[4:5])

    logits = jnp.dot(h.astype(BF16), rw_ref[...], preferred_element_type=F32) + rb_ref[...]
    lane = lax.broadcasted_iota(jnp.int32, logits.shape, 1)
    logits = jnp.where(lane < N_EXPERTS, logits, -jnp.inf)
    v1 = jnp.max(logits, axis=-1, keepdims=True)
    i1 = jnp.min(jnp.where(logits == v1, lane, LANES), axis=-1, keepdims=True)
    rest = jnp.where(lane == i1, -jnp.inf, logits)
    v2 = jnp.max(rest, axis=-1, keepdims=True)
    i2 = jnp.min(jnp.where(rest == v2, lane, LANES), axis=-1, keepdims=True)
    e2 = jnp.exp(v2 - v1)
    den = 1.0 + e2
    route = jnp.where(lane == 0, i1.astype(F32),
                      jnp.where(lane == 1, i2.astype(F32),
                                jnp.where(lane == 2, 1.0 / den,
                                          jnp.where(lane == 3, e2 / den, 0.0))))
    route_out[0] = route
    ids_out[0] = jnp.transpose(route[:, 0:SUBLANES])


def _oproj_router(o, x, mod, g, wo, rw, rb, tm):
    bsz, s, d = x.shape
    nt = s // tm
    row = lambda b, j: (0, 0)
    tile = lambda b, j: (b, j, 0)
    return pl.pallas_call(
        _oproj_router_kernel,
        grid=(bsz, s // tm),
        in_specs=[pl.BlockSpec((1, tm, d), tile),
                  pl.BlockSpec((1, tm, d), tile),
                  pl.BlockSpec((1, 6, d), lambda b, j: (b, 0, 0)),
                  pl.BlockSpec((1, d), row),
                  pl.BlockSpec((d, d), row),
                  pl.BlockSpec((d, LANES), row),
                  pl.BlockSpec((1, LANES), row)],
        out_specs=[pl.BlockSpec((1, tm, d), tile),
                   pl.BlockSpec((1, tm, LANES), tile),
                   pl.BlockSpec((1, SUBLANES, tm), lambda b, j: (b * nt + j, 0, 0))],
        out_shape=[jax.ShapeDtypeStruct((bsz, s, d), F32),
                   jax.ShapeDtypeStruct((bsz, s, LANES), F32),
                   jax.ShapeDtypeStruct((bsz * nt, SUBLANES, tm), F32)],
        compiler_params=_cparams(("arbitrary", "arbitrary")),
        name="oproj_router",
    )(o, x, mod, g.reshape(1, d), wo, rw, rb)


DMA_UNROLL = 8


def _pack_bf16_pairs(h):
    half = h.shape[1] // 2
    hb = h.astype(BF16).astype(F32)
    lo = lax.shift_right_logical(pltpu.bitcast(hb[:, :half], jnp.uint32), jnp.uint32(16))
    hi = pltpu.bitcast(hb[:, half:], jnp.uint32) & jnp.uint32(0xFFFF0000)
    return lo | hi


def _unpack_bf16_pairs(w):
    lo = pltpu.bitcast(lax.shift_left(w, jnp.uint32(16)), F32)
    hi = pltpu.bitcast(w & jnp.uint32(0xFFFF0000), F32)
    return lo.astype(BF16), hi.astype(BF16)


def _dispatch_copy(slot_ref, hbuf, hs_hbm, sem, r, k):
    return pltpu.make_async_copy(hbuf.at[pl.ds(r, 1)], hs_hbm.at[pl.ds(slot_ref[0, 0, TOP_K * r + k], 1)], sem)


def _pad_copy(pad_ref, zbuf, hs_hbm, sem, e):
    return pltpu.make_async_copy(zbuf, hs_hbm.at[pl.ds(pl.multiple_of(pad_ref[e], SUBLANES), zbuf.shape[0])], sem)


def _dispatch_kernel(slot_ref, pad_ref, x_ref, mod_ref, g_ref, hs_out, hbuf, zbuf, sem, pad_sem):
    tm = x_ref.shape[1]

    @pl.when((pl.program_id(0) == 0) & (pl.program_id(1) == 0))
    def _():
        zbuf[...] = jnp.zeros(zbuf.shape, zbuf.dtype)
        for e in range(N_EXPERTS):
            _pad_copy(pad_ref, zbuf, hs_out, pad_sem, e).start()
        for e in range(N_EXPERTS):
            _pad_copy(pad_ref, zbuf, hs_out, pad_sem, e).wait()

        def clear_tile(t, c):
            tile = pltpu.make_async_copy(
                zbuf, hs_out.at[pl.ds(pl.multiple_of(t * zbuf.shape[0], zbuf.shape[0]), zbuf.shape[0])], pad_sem)
            tile.start()
            tile.wait()
            return c

        lax.fori_loop(pad_ref[N_EXPERTS], hs_out.shape[0] // zbuf.shape[0], clear_tile, 0)

    mod = mod_ref[0]
    hbuf[...] = _pack_bf16_pairs(_rms_mod(x_ref[0], g_ref[...], mod[3:4], mod[4:5]))

    def start(r, c):
        for k in range(TOP_K):
            _dispatch_copy(slot_ref, hbuf, hs_out, sem, r, k).start()
        return c

    def wait(r, c):
        for k in range(TOP_K):
            _dispatch_copy(slot_ref, hbuf, hs_out, sem, r, k).wait()
        return c

    lax.fori_loop(0, tm, start, 0, unroll=DMA_UNROLL)
    lax.fori_loop(0, tm, wait, 0, unroll=DMA_UNROLL)


def _dispatch(slots, pad_start, x, mod, g, n_slots, tm, pad_rows):
    bsz, s, d = x.shape
    nt = s // tm
    return pl.pallas_call(
        _dispatch_kernel,
        grid=(bsz, nt),
        in_specs=[pl.BlockSpec((1, 1, TOP_K * tm), lambda b, j: (b * nt + j, 0, 0), memory_space=pltpu.SMEM),
                  pl.BlockSpec(memory_space=pltpu.SMEM),
                  pl.BlockSpec((1, tm, d), lambda b, j: (b, j, 0)),
                  pl.BlockSpec((1, 6, d), lambda b, j: (b, 0, 0)),
                  pl.BlockSpec((1, d), lambda b, j: (0, 0))],
        out_specs=pl.BlockSpec(memory_space=pl.ANY),
        out_shape=jax.ShapeDtypeStruct((n_slots, d // 2), jnp.uint32),
        scratch_shapes=[pltpu.VMEM((tm, d // 2), jnp.uint32), pltpu.VMEM((pad_rows, d // 2), jnp.uint32),
                        pltpu.SemaphoreType.DMA(()), pltpu.SemaphoreType.DMA(())],
        compiler_params=_cparams(("arbitrary", "arbitrary")),
        name="moe_dispatch",
    )(slots.reshape(bsz * nt, 1, TOP_K * tm), pad_start, x, mod, g.reshape(1, d))


def _moe_kernel(te_ref, nv_ref, hs_ref, wa_ref, wg_ref, w2_ref, o_ref, h_scr, acc_scr):
    del te_ref
    i = pl.program_id(0)
    j = pl.program_id(1)
    valid = i < nv_ref[0]

    @pl.when(valid & (j == 0))
    def _():
        half = hs_ref.shape[1]
        lo, hi = _unpack_bf16_pairs(hs_ref[...])
        h_scr[:, :half] = lo
        h_scr[:, half:] = hi
        acc_scr[...] = jnp.zeros_like(acc_scr)

    @pl.when(valid)
    def _():
        h = h_scr[...]
        a = jnp.dot(h, wa_ref[0].astype(BF16), preferred_element_type=F32)
        g = jnp.dot(h, wg_ref[0].astype(BF16), preferred_element_type=F32)
        acc_scr[...] += jnp.dot((_silu(a) * g).astype(BF16), w2_ref[0].astype(BF16), preferred_element_type=F32)

    last = j == pl.num_programs(1) - 1

    @pl.when(valid & last)
    def _():
        o_ref[...] = acc_scr[...]

    @pl.when(jnp.logical_not(valid) & last)
    def _():
        o_ref[...] = jnp.zeros_like(o_ref)


def _moe_ffn(tile_expert, n_valid, hs, w13, w2, tm, tf):
    n_slots, half = hs.shape
    d = 2 * half
    f = w2.shape[1]
    nf = f // tf
    nt = n_slots // tm

    def row_idx(i, j, te, nv):
        return (jnp.minimum(i, nv[0] - 1), 0)

    def ff(i, j, nv):
        return jnp.where(i < nv[0], j, nf - 1)

    grid_spec = pltpu.PrefetchScalarGridSpec(
        num_scalar_prefetch=2,
        grid=(nt, nf),
        in_specs=[pl.BlockSpec((tm, half), row_idx),
                  pl.BlockSpec((1, d, tf), lambda i, j, te, nv: (te[i], 0, ff(i, j, nv))),
                  pl.BlockSpec((1, d, tf), lambda i, j, te, nv: (te[i], 0, nf + ff(i, j, nv))),
                  pl.BlockSpec((1, tf, d), lambda i, j, te, nv: (te[i], ff(i, j, nv), 0))],
        out_specs=pl.BlockSpec((tm, d), lambda i, j, te, nv: (i, 0)),
        scratch_shapes=[pltpu.VMEM((tm, d), BF16), pltpu.VMEM((tm, d), F32)],
    )
    return pl.pallas_call(
        _moe_kernel,
        grid_spec=grid_spec,
        out_shape=jax.ShapeDtypeStruct((n_slots, d), F32),
        compiler_params=_cparams(("arbitrary", "arbitrary")),
        name="moe_ffn",
    )(tile_expert, n_valid, hs, w13, w13, w2)


def _combine_copy(slot_ref, ys_hbm, buf, sems, par, r, k):
    return pltpu.make_async_copy(ys_hbm.at[pl.ds(slot_ref[0, 0, TOP_K * r + k], 1)],
                                 buf.at[par, k, pl.ds(r, 1)], sems.at[par])


def _combine_kernel(slot_ref, next_slot_ref, x_ref, route_ref, mod_ref, g_ref, ys_hbm, o_ref, buf, sems):
    tc = x_ref.shape[1]
    g = pl.program_id(0) * pl.num_programs(1) + pl.program_id(1)
    n_steps = pl.num_programs(0) * pl.num_programs(1)
    par = lax.rem(g, 2)

    def fetch(slots, parity):
        def start(r, c):
            for k in range(TOP_K):
                _combine_copy(slots, ys_hbm, buf, sems, parity, r, k).start()
            return c
        lax.fori_loop(0, tc, start, 0, unroll=DMA_UNROLL)

    @pl.when(g == 0)
    def _():
        fetch(slot_ref, par)

    @pl.when(g + 1 < n_steps)
    def _():
        fetch(next_slot_ref, 1 - par)

    def wait(r, c):
        for k in range(TOP_K):
            _combine_copy(slot_ref, ys_hbm, buf, sems, par, r, k).wait()
        return c

    lax.fori_loop(0, tc, wait, 0, unroll=DMA_UNROLL)

    route = route_ref[0]
    y = route[:, 2:3] * buf[par, 0] + route[:, 3:4] * buf[par, 1]
    x = x_ref[0] + mod_ref[0][5:6] * y
    o_ref[0] = (x * lax.rsqrt(jnp.mean(x * x, axis=-1, keepdims=True) + NORM_EPS)) * g_ref[...]


def _combine(slots, x, route, mod, g, ys, tc):
    bsz, s, d = x.shape
    nt = s // tc
    tile = lambda b, j: (b, j, 0)
    slots = slots.reshape(bsz * nt, 1, TOP_K * tc)
    return pl.pallas_call(
        _combine_kernel,
        grid=(bsz, nt),
        in_specs=[pl.BlockSpec((1, 1, TOP_K * tc), lambda b, j: (b * nt + j, 0, 0), memory_space=pltpu.SMEM),
                  pl.BlockSpec((1, 1, TOP_K * tc), lambda b, j: (jnp.minimum(b * nt + j + 1, bsz * nt - 1), 0, 0),
                               memory_space=pltpu.SMEM),
                  pl.BlockSpec((1, tc, d), tile),
                  pl.BlockSpec((1, tc, LANES), tile),
                  pl.BlockSpec((1, 6, d), lambda b, j: (b, 0, 0)),
                  pl.BlockSpec((1, d), lambda b, j: (0, 0)),
                  pl.BlockSpec(memory_space=pl.ANY)],
        out_specs=pl.BlockSpec((1, tc, d), tile),
        out_shape=jax.ShapeDtypeStruct((bsz, s, d), F32),
        scratch_shapes=[pltpu.VMEM((2, TOP_K, tc, d), F32), pltpu.SemaphoreType.DMA((2,))],
        compiler_params=_cparams(("arbitrary", "arbitrary")),
        name="moe_combine",
    )(slots, slots, x, route, mod, g.reshape(1, d), ys)


def _routing_plan(ids, tm):
    t = ids.shape[0] * ids.shape[2]
    flat_e = ids[:, :TOP_K, :].astype(jnp.int32).transpose(0, 2, 1).reshape(-1)
    onehot = (flat_e[:, None] == jnp.arange(N_EXPERTS, dtype=jnp.int32)[None, :]).astype(jnp.int32)
    csum = jnp.cumsum(onehot, axis=0)
    rank = jnp.sum((csum - onehot) * onehot, axis=1)
    counts = csum[-1]
    padded = ((counts + tm - 1) // tm) * tm
    ends = jnp.cumsum(padded)
    starts = ends - padded
    slots = jnp.sum(starts[None, :] * onehot, axis=1) + rank
    n_tiles = TOP_K * t // tm + N_EXPERTS
    tile_start = jnp.arange(n_tiles, dtype=jnp.int32) * tm
    n_valid = (ends[-1] // tm).astype(jnp.int32)
    tile_expert = jnp.sum((tile_start[:, None] >= ends[None, :]).astype(jnp.int32), axis=1)
    last_expert = jnp.sum((ends[-1] - 1 >= ends).astype(jnp.int32))
    tile_expert = jnp.where(tile_start < ends[-1], tile_expert, last_expert).astype(jnp.int32)
    pad_start = ((starts + counts) // SUBLANES * SUBLANES).astype(jnp.int32)
    pad_start = jnp.concatenate([pad_start, n_valid.reshape(1)])
    return slots.astype(jnp.int32), tile_expert, n_valid.reshape(1), n_tiles, pad_start


def kernel(x, c, positions, ada_w, ada_b, norm1_g, norm2_g, conv_w1, conv_b1, conv_dw_w, conv_dw_b,
           conv_ln_g, conv_ln_b, conv_w2, conv_b2, kv_ada_w, kv_ada_b, kv_norm_g, w_kv, w_q, w_o,
           ffn_w13, ffn_w2, router_w, router_b, moe_w13, moe_w2, final_g):
    bsz, s, d = x.shape
    assert ada_w.shape[0] == 2 and d == N_HEADS * HEAD_DIM and s % MOBA_BLOCK == 0
    nb = s // MOBA_BLOCK
    topk = min(MOBA_TOPK, max(nb - 1, 1))
    tm, moe_tm = ROW_TILE, MOE_ROW_TILE
    assert s % tm == 0 and tm % MOBA_BLOCK == 0 and (TOP_K * bsz * s) % moe_tm == 0

    mod0 = _ada(c, ada_w[0], ada_b[0]).reshape(bsz, 6, d)
    mod1 = _ada(c, ada_w[1], ada_b[1]).reshape(bsz, 6, d)
    modkv = _ada(c, kv_ada_w, kv_ada_b).reshape(bsz, 2, d)

    x = _conv_mixer(x, mod0, norm1_g[0], conv_w1[0].astype(BF16), conv_b1[0], conv_dw_w[0], conv_dw_b[0],
                    conv_ln_g[0], conv_ln_b[0], conv_w2[0].astype(BF16), conv_b2[0], tm)
    x = _dense_ffn(x, mod0, norm2_g[0], ffn_w13[0].astype(BF16), ffn_w2[0].astype(BF16), tm, FFN_CHUNK)

    inv_freq = ROPE_THETA ** (-jnp.arange(0, ROT_DIM, 2, dtype=F32) / ROT_DIM)
    freq = jnp.tile(inv_freq, HEAD_DIM // (ROT_DIM // 2)).reshape(1, HEAD_DIM)
    q, k, vt, kmean = _kvq(x, modkv, mod1, kv_norm_g, norm1_g[1], w_kv.astype(BF16), w_q[0].astype(BF16),
                          positions.reshape(bsz, s, 1), freq, tm)
    kmean = kmean.reshape(bsz, nb, N_HEADS, HEAD_DIM).transpose(0, 2, 1, 3).astype(BF16)
    kmean = jnp.pad(kmean, ((0, 0), (0, 0), (0, -nb % BF16_ROWS), (0, 0)))

    o = _attention(q, k, vt, kmean, topk)
    rw = jnp.zeros((d, LANES), BF16).at[:, :N_EXPERTS].set(router_w[0].astype(BF16))
    rb = jnp.zeros((1, LANES), F32).at[0, :N_EXPERTS].set(router_b[0])
    x, route, ids = _oproj_router(o, x, mod1, norm2_g[1], w_o[0].astype(BF16), rw, rb, tm)

    slots, tile_expert, n_valid, n_tiles, pad_start = _routing_plan(ids, moe_tm)
    hs = _dispatch(slots, pad_start, x, mod1, norm2_g[1], n_tiles * moe_tm, tm, moe_tm)
    ys = _moe_ffn(tile_expert, n_valid, hs, moe_w13[0], moe_w2[0], moe_tm, MOE_FF_TILE)
    return _combine(slots, x, route, mod1, final_g, ys, tm)
```

```python
import functools

import jax
import jax.numpy as jnp
from jax import lax
from jax.experimental import pallas as pl
from jax.experimental.pallas import tpu as pltpu

N_HEADS = 8
HEAD_DIM = 128
ROT_DIM = HEAD_DIM // 4
ROPE_THETA = 500000.0
MOBA_BLOCK = 256
MOBA_TOPK = 3
CONV_WIDTH = 31
N_EXPERTS = 8
TOP_K = 2
NORM_EPS = 1e-6
LOG2_E = 1.4426950408889634

LANES = 128
SUBLANES = 8
BF16_ROWS = 16
CONV_HALO = 32
VMEM_LIMIT = 56 * 1024 * 1024

ROW_TILE = 512
FFN_CHUNK = 512
MOE_ROW_TILE = 1024
MOE_FF_TILE = 512

F32 = jnp.float32
BF16 = jnp.bfloat16
HIGHEST = lax.Precision.HIGHEST


def _cparams(sem):
    return pltpu.CompilerParams(dimension_semantics=sem, vmem_limit_bytes=VMEM_LIMIT)


def _rms_mod(x, g, shift, scale):
    y = x * lax.rsqrt(jnp.mean(x * x, axis=-1, keepdims=True) + NORM_EPS)
    return (y * g) * (1.0 + scale) + shift


def _silu(a):
    return a * jax.nn.sigmoid(a)


def _ada_kernel(c_ref, w_ref, b_ref, o_ref):
    o_ref[...] = jnp.dot(_silu(c_ref[...]), w_ref[0], preferred_element_type=F32,
                         precision=HIGHEST) + b_ref[...]


def _ada(c, w, layer, b):
    bsz, d = c.shape
    n = w.shape[2]
    tn = min(n, 2048)
    return pl.pallas_call(
        _ada_kernel,
        grid=(n // tn,),
        in_specs=[pl.BlockSpec((bsz, d), lambda j: (0, 0)),
                  pl.BlockSpec((1, d, tn), lambda j: (layer, 0, j)),
                  pl.BlockSpec((1, tn), lambda j: (0, j))],
        out_specs=pl.BlockSpec((bsz, tn), lambda j: (0, j)),
        out_shape=jax.ShapeDtypeStruct((bsz, n), F32),
        compiler_params=_cparams(("arbitrary",)),
        name="ada",
    )(c, w, b.reshape(1, n))


CONV_ROWS = 128


def _conv_mixer_kernel(x_ref, mod_ref, g_ref, w1_ref, b1_ref, dw_ref, dwb_ref, lng_ref, lnb_ref,
                       w2_ref, b2_ref, o_ref, ubuf, zbuf):
    tm, d = x_ref.shape[1], x_ref.shape[2]
    j = pl.program_id(1)

    @pl.when(j == 0)
    def _():
        ubuf[0:CONV_HALO, :] = jnp.zeros((CONV_HALO, d), F32)

    @pl.when(j > 0)
    def _():
        ubuf[0:CONV_HALO, :] = ubuf[tm:tm + CONV_HALO, :]

    x = x_ref[0]
    mod = mod_ref[0]
    h = _rms_mod(x, g_ref[...], mod[0:1], mod[1:2]).astype(BF16)
    ag = jnp.dot(h, w1_ref[...], preferred_element_type=F32) + b1_ref[...]
    ubuf[CONV_HALO:CONV_HALO + tm, :] = ag[:, :d] * jax.nn.sigmoid(ag[:, d:])

    lead = CONV_HALO - (CONV_WIDTH - 1)
    win_rows = CONV_ROWS + CONV_HALO

    def chunk(r, carry):
        base = pl.multiple_of(r * CONV_ROWS, CONV_ROWS)
        for lc in range(d // LANES):
            lanes = slice(lc * LANES, (lc + 1) * LANES)
            win = ubuf[pl.ds(base, win_rows), lanes]
            acc = jnp.broadcast_to(dwb_ref[:, lanes], (CONV_ROWS, LANES))
            for phase in range(SUBLANES):
                taps = [k for k in range(CONV_WIDTH) if (lead + k) % SUBLANES == phase]
                if not taps:
                    continue
                shifted = win if phase == 0 else pltpu.roll(win, win_rows - phase, 0)
                for k in taps:
                    off = lead + k - phase
                    acc = acc + dw_ref[k:k + 1, lanes] * shifted[off:off + CONV_ROWS, :]
            zbuf[pl.ds(base, CONV_ROWS), lanes] = acc
        return carry

    lax.fori_loop(0, tm // CONV_ROWS, chunk, 0)

    z = zbuf[...]
    mu = jnp.mean(z, axis=-1, keepdims=True)
    zc = z - mu
    var = jnp.mean(zc * zc, axis=-1, keepdims=True)
    zn = (zc * lax.rsqrt(var + NORM_EPS)) * lng_ref[...] + lnb_ref[...]
    mix = jnp.dot(_silu(zn).astype(BF16), w2_ref[...], preferred_element_type=F32) + b2_ref[...]
    o_ref[0] = x + mod[2:3] * mix


def _conv_mixer(x, mod, g, w1, b1, dw, dwb, lng, lnb, w2, b2, tm):
    bsz, s, d = x.shape
    row = lambda b, j: (0, 0)
    return pl.pallas_call(
        _conv_mixer_kernel,
        grid=(bsz, s // tm),
        in_specs=[pl.BlockSpec((1, tm, d), lambda b, j: (b, j, 0)),
                  pl.BlockSpec((1, 6, d), lambda b, j: (b, 0, 0)),
                  pl.BlockSpec((1, d), row),
                  pl.BlockSpec((d, 2 * d), row),
                  pl.BlockSpec((1, 2 * d), row),
                  pl.BlockSpec((CONV_WIDTH, d), row),
                  pl.BlockSpec((1, d), row),
                  pl.BlockSpec((1, d), row),
                  pl.BlockSpec((1, d), row),
                  pl.BlockSpec((d, d), row),
                  pl.BlockSpec((1, d), row)],
        out_specs=pl.BlockSpec((1, tm, d), lambda b, j: (b, j, 0)),
        out_shape=jax.ShapeDtypeStruct((bsz, s, d), F32),
        scratch_shapes=[pltpu.VMEM((tm + CONV_HALO, d), F32), pltpu.VMEM((tm, d), F32)],
        compiler_params=_cparams(("arbitrary", "arbitrary")),
        name="conv_mixer",
    )(x, mod, g.reshape(1, d), w1, b1.reshape(1, 2 * d), dw, dwb.reshape(1, d), lng.reshape(1, d),
      lnb.reshape(1, d), w2, b2.reshape(1, d))


def _ffn_kernel(x_ref, mod_ref, g_ref, w13_ref, w2_ref, o_ref, *, chunk):
    f = w2_ref.shape[0]
    x = x_ref[0]
    mod = mod_ref[0]
    h = _rms_mod(x, g_ref[...], mod[3:4], mod[4:5]).astype(BF16)
    acc = None
    for c0 in range(0, f, chunk):
        c1 = min(c0 + chunk, f)
        a = jnp.dot(h, w13_ref[:, c0:c1], preferred_element_type=F32)
        g = jnp.dot(h, w13_ref[:, f + c0:f + c1], preferred_element_type=F32)
        part = jnp.dot((_silu(a) * g).astype(BF16), w2_ref[c0:c1, :], preferred_element_type=F32)
        acc = part if acc is None else acc + part
    o_ref[0] = x + mod[5:6] * acc


def _dense_ffn(x, mod, g, w13, w2, tm, chunk):
    bsz, s, d = x.shape
    f = w2.shape[0]
    const = lambda b, i: (0, 0)
    return pl.pallas_call(
        functools.partial(_ffn_kernel, chunk=chunk),
        grid=(bsz, s // tm),
        in_specs=[pl.BlockSpec((1, tm, d), lambda b, i: (b, i, 0)),
                  pl.BlockSpec((1, 6, d), lambda b, i: (b, 0, 0)),
                  pl.BlockSpec((1, d), const),
                  pl.BlockSpec((d, 2 * f), const, pipeline_mode=pl.Buffered(1)),
                  pl.BlockSpec((f, d), const, pipeline_mode=pl.Buffered(1))],
        out_specs=pl.BlockSpec((1, tm, d), lambda b, i: (b, i, 0)),
        out_shape=jax.ShapeDtypeStruct((bsz, s, d), F32),
        compiler_params=_cparams(("arbitrary", "arbitrary")),
        name="dense_ffn",
    )(x, mod, g.reshape(1, d), w13, w2)


def _rotary_heads(y, cos, sin_signed, low_half):
    outs = []
    for hh in range(y.shape[1] // HEAD_DIM):
        ys = y[:, hh * HEAD_DIM:(hh + 1) * HEAD_DIM]
        partner = jnp.where(low_half, pltpu.roll(ys, HEAD_DIM - ROT_DIM // 2, 1),
                            pltpu.roll(ys, ROT_DIM // 2, 1))
        outs.append(ys * cos + partner * sin_signed)
    return outs


def _kvq_kernel(x_ref, modkv_ref, mod_ref, gkv_ref, gq_ref, wkv_ref, wq_ref, pos_ref, freq_ref,
                q_ref, k_ref, vt_ref, km_ref):
    tm, d = x_ref.shape[1], x_ref.shape[2]
    modkv = modkv_ref[0]
    mod = mod_ref[0]
    lane = lax.broadcasted_iota(jnp.int32, (MOBA_BLOCK, HEAD_DIM), 1)
    low_half = lane < ROT_DIM // 2

    for r in range(tm // MOBA_BLOCK):
        rows = slice(r * MOBA_BLOCK, (r + 1) * MOBA_BLOCK)
        x = x_ref[0, rows, :]
        xn = x * lax.rsqrt(jnp.mean(x * x, axis=-1, keepdims=True) + NORM_EPS)
        hkv = ((xn * gkv_ref[...]) * (1.0 + modkv[1:2]) + modkv[0:1]).astype(BF16)
        hq = ((xn * gq_ref[...]) * (1.0 + mod[1:2]) + mod[0:1]).astype(BF16)
        kv = jnp.dot(hkv, wkv_ref[...], preferred_element_type=F32)
        qf = jnp.dot(hq, wq_ref[...], preferred_element_type=F32)

        n_grp = LANES // ROT_DIM
        grp = MOBA_BLOCK // n_grp
        lane_grp = lax.broadcasted_iota(jnp.int32, (grp, LANES), 1) // ROT_DIM
        packed = None
        for g in range(n_grp):
            first = r * MOBA_BLOCK + g * grp
            pos_g = pos_ref[0, first:first + grp, :].astype(F32)
            packed = pos_g if packed is None else jnp.where(lane_grp == g, pos_g, packed)
        ang = packed * freq_ref[...]
        cos_p = jnp.cos(ang)
        sin_p = jnp.sin(ang)
        rot = lax.broadcasted_iota(jnp.int32, (grp, LANES), 1) < ROT_DIM
        cos_parts, sin_parts = [], []
        for g in range(n_grp):
            shift = (LANES - g * ROT_DIM) % LANES
            cos_g = cos_p if g == 0 else pltpu.roll(cos_p, shift, 1)
            sin_g = sin_p if g == 0 else pltpu.roll(sin_p, shift, 1)
            cos_parts.append(jnp.where(rot, cos_g, 1.0))
            sin_parts.append(jnp.where(rot, sin_g, 0.0))
        cos = jnp.concatenate(cos_parts, axis=0)
        sin = jnp.concatenate(sin_parts, axis=0)
        sin_signed = jnp.where(low_half, -sin, sin)

        vt_ref[0, :, rows] = kv[:, d:].T.astype(BF16)
        k_heads = _rotary_heads(kv[:, :d], cos, sin_signed, low_half)
        q_heads = _rotary_heads(qf, cos, sin_signed, low_half)
        for hh in range(N_HEADS):
            hs = slice(hh * HEAD_DIM, (hh + 1) * HEAD_DIM)
            q_ref[0, rows, hs] = q_heads[hh].astype(BF16)
            k_ref[0, rows, hs] = k_heads[hh].astype(BF16)
            km_ref[0, r, :, hs] = jnp.mean(k_heads[hh], axis=0, keepdims=True)


def _kvq(x, modkv, mod, gkv, gq, wkv, wq, pos, freq, tm):
    bsz, s, d = x.shape
    nb = s // MOBA_BLOCK
    row = lambda b, j: (0, 0)
    act = jax.ShapeDtypeStruct((bsz, s, d), BF16)
    return pl.pallas_call(
        _kvq_kernel,
        grid=(bsz, s // tm),
        in_specs=[pl.BlockSpec((1, tm, d), lambda b, j: (b, j, 0)),
                  pl.BlockSpec((1, 2, d), lambda b, j: (b, 0, 0)),
                  pl.BlockSpec((1, 6, d), lambda b, j: (b, 0, 0)),
                  pl.BlockSpec((1, d), row),
                  pl.BlockSpec((1, d), row),
                  pl.BlockSpec((d, 2 * d), row),
                  pl.BlockSpec((d, d), row),
                  pl.BlockSpec((1, tm, 1), lambda b, j: (b, j, 0)),
                  pl.BlockSpec((1, HEAD_DIM), row)],
        out_specs=[pl.BlockSpec((1, tm, d), lambda b, j: (b, j, 0)),
                   pl.BlockSpec((1, tm, d), lambda b, j: (b, j, 0)),
                   pl.BlockSpec((1, d, tm), lambda b, j: (b, 0, j)),
                   pl.BlockSpec((1, tm // MOBA_BLOCK, 1, d), lambda b, j: (b, j, 0, 0))],
        out_shape=[act, act, jax.ShapeDtypeStruct((bsz, d, s), BF16),
                   jax.ShapeDtypeStruct((bsz, nb, 1, d), F32)],
        compiler_params=_cparams(("arbitrary", "arbitrary")),
        name="kvq_proj",
    )(x, modkv, mod, gkv.reshape(1, d), gq.reshape(1, d), wkv, wq, pos, freq)


def _attn_scores(own, nb, q, km, k_ref, s_scr, topk):
    contract_last = (((1,), (1,)), ((), ()))
    tq = q.shape[0]

    k_own = k_ref[0, own * MOBA_BLOCK:(own + 1) * MOBA_BLOCK, :]
    ext = lax.dot_general(jnp.concatenate([k_own, km], axis=0), q, contract_last,
                          preferred_element_type=F32)
    s = ext[:MOBA_BLOCK]
    causal = lax.broadcasted_iota(jnp.int32, s.shape, 0) <= lax.broadcasted_iota(jnp.int32, s.shape, 1)
    s = jnp.where(causal, s, -jnp.inf)
    s_scr[own] = s
    m = jnp.max(s, axis=0, keepdims=True)

    sel = None
    if own > topk:
        blk = lax.broadcasted_iota(jnp.int32, (nb, tq), 0)
        past = blk < own
        gate = jnp.where(past, ext[MOBA_BLOCK:MOBA_BLOCK + nb], -jnp.inf)
        rank = jnp.zeros((nb, tq), jnp.int32)
        for j in range(own):
            gj = gate[j:j + 1, :]
            rank = rank + jnp.where(gj > gate, 1, jnp.where((gj == gate) & (blk > j), 1, 0))
        sel = jnp.where(past & (rank < topk), 1.0, 0.0)

    for n in range(own):
        kn = k_ref[0, n * MOBA_BLOCK:(n + 1) * MOBA_BLOCK, :]
        s = lax.dot_general(kn, q, contract_last, preferred_element_type=F32)
        if sel is not None:
            s = jnp.where(sel[n:n + 1, :] > 0.0, s, -jnp.inf)
        s_scr[n] = s
        m = jnp.maximum(m, jnp.max(s, axis=0, keepdims=True))
    return m


def _attn_output(own, m, vt_ref, o_ref, s_scr, p_scr):
    c = HEAD_DIM ** -0.5 * LOG2_E
    l = None
    for n in range(own + 1):
        p = jnp.exp2((s_scr[n] - m) * c)
        bl = jnp.sum(p, axis=0, keepdims=True)
        l = bl if l is None else l + bl
        p_scr[n * MOBA_BLOCK:(n + 1) * MOBA_BLOCK, :] = p.astype(BF16)
    span = (own + 1) * MOBA_BLOCK
    acc = jnp.dot(vt_ref[0, :, :span], p_scr[:span, :], preferred_element_type=F32)
    o_ref[0] = (acc / l).T.astype(o_ref.dtype)


def _attn_kernel(q_ref, k_ref, vt_ref, km_ref, o_ref, s_scr, p_scr, *, topk):
    nb = q_ref.shape[1] // MOBA_BLOCK
    km = km_ref[0, 0]

    def rows(own):
        return slice(own * MOBA_BLOCK, (own + 1) * MOBA_BLOCK)

    def blocks(own):
        first = own * (own + 1) // 2
        return slice(first, first + own + 1), slice(first * MOBA_BLOCK, (first + own + 1) * MOBA_BLOCK)

    def scores(own):
        return _attn_scores(own, nb, q_ref[0, rows(own), :], km, k_ref, s_scr.at[blocks(own)[0]], topk)

    m = scores(0)
    for own in range(nb):
        m_next = scores(own + 1) if own + 1 < nb else None
        _attn_output(own, m, vt_ref, o_ref.at[:, rows(own), :], s_scr.at[blocks(own)[0]], p_scr.at[blocks(own)[1]])
        m = m_next


def _attention(q, k, vt, kmean, topk):
    bsz, s, d = q.shape
    nb = s // MOBA_BLOCK
    n_pairs = nb * (nb + 1) // 2
    return pl.pallas_call(
        functools.partial(_attn_kernel, topk=topk),
        grid=(bsz, N_HEADS),
        in_specs=[pl.BlockSpec((1, s, HEAD_DIM), lambda b, h: (b, 0, h)),
                  pl.BlockSpec((1, s, HEAD_DIM), lambda b, h: (b, 0, h)),
                  pl.BlockSpec((1, HEAD_DIM, s), lambda b, h: (b, h, 0)),
                  pl.BlockSpec((1, 1, kmean.shape[2], HEAD_DIM), lambda b, h: (b, h, 0, 0))],
        out_specs=pl.BlockSpec((1, s, HEAD_DIM), lambda b, h: (b, 0, h)),
        out_shape=jax.ShapeDtypeStruct((bsz, s, d), BF16),
        scratch_shapes=[pltpu.VMEM((n_pairs, MOBA_BLOCK, MOBA_BLOCK), F32),
                        pltpu.VMEM((n_pairs * MOBA_BLOCK, MOBA_BLOCK), BF16)],
        compiler_params=_cparams(("arbitrary", "arbitrary")),
        name="moba_attention",
    )(q, k, vt, kmean)


def _oproj_router_kernel(o_ref, x_ref, mod_ref, g_ref, wo_ref, rw_ref, rb_ref, x_out, route_out, ids_out):
    mod = mod_ref[0]
    mix = jnp.dot(o_ref[0], wo_ref[...], preferred_element_type=F32)
    x = x_ref[0] + mod[2:3] * mix
    x_out[0] = x
    h = _rms_mod(x, g_ref[...], mod[3:4], mod[4:5])

    logits = jnp.dot(h.astype(BF16), rw_ref[...], preferred_element_type=F32) + rb_ref[...]
    lane = lax.broadcasted_iota(jnp.int32, logits.shape, 1)
    logits = jnp.where(lane < N_EXPERTS, logits, -jnp.inf)
    v1 = jnp.max(logits, axis=-1, keepdims=True)
    i1 = jnp.min(jnp.where(logits == v1, lane, LANES), axis=-1, keepdims=True)
    rest = jnp.where(lane == i1, -jnp.inf, logits)
    v2 = jnp.max(rest, axis=-1, keepdims=True)
    i2 = jnp.min(jnp.where(rest == v2, lane, LANES), axis=-1, keepdims=True)
    e2 = jnp.exp(v2 - v1)
    den = 1.0 + e2
    route = jnp.where(lane == 0, i1.astype(F32),
                      jnp.where(lane == 1, i2.astype(F32),
                                jnp.where(lane == 2, 1.0 / den,
                                          jnp.where(lane == 3, e2 / den, 0.0))))
    route_out[0] = route
    ids_out[0] = jnp.transpose(route[:, 0:SUBLANES])


def _oproj_router(o, x, mod, g, wo, rw, rb, tm):
    bsz, s, d = x.shape
    nt = s // tm
    row = lambda b, j: (0, 0)
    tile = lambda b, j: (b, j, 0)
    return pl.pallas_call(
        _oproj_router_kernel,
        grid=(bsz, s // tm),
        in_specs=[pl.BlockSpec((1, tm, d), tile),
                  pl.BlockSpec((1, tm, d), tile),
                  pl.BlockSpec((1, 6, d), lambda b, j: (b, 0, 0)),
                  pl.BlockSpec((1, d), row),
                  pl.BlockSpec((d, d), row),
                  pl.BlockSpec((d, LANES), row),
                  pl.BlockSpec((1, LANES), row)],
        out_specs=[pl.BlockSpec((1, tm, d), tile),
                   pl.BlockSpec((1, tm, LANES), tile),
                   pl.BlockSpec((1, SUBLANES, tm), lambda b, j: (b * nt + j, 0, 0))],
        out_shape=[jax.ShapeDtypeStruct((bsz, s, d), F32),
                   jax.ShapeDtypeStruct((bsz, s, LANES), F32),
                   jax.ShapeDtypeStruct((bsz * nt, SUBLANES, tm), F32)],
        compiler_params=_cparams(("arbitrary", "arbitrary")),
        name="oproj_router",
    )(o, x, mod, g.reshape(1, d), wo, rw, rb)


DMA_UNROLL = 8


def _pack_bf16_pairs(h):
    half = h.shape[1] // 2
    hb = h.astype(BF16).astype(F32)
    lo = lax.shift_right_logical(pltpu.bitcast(hb[:, :half], jnp.uint32), jnp.uint32(16))
    hi = pltpu.bitcast(hb[:, half:], jnp.uint32) & jnp.uint32(0xFFFF0000)
    return lo | hi


def _unpack_bf16_pairs(w):
    lo = pltpu.bitcast(lax.shift_left(w, jnp.uint32(16)), F32)
    hi = pltpu.bitcast(w & jnp.uint32(0xFFFF0000), F32)
    return lo.astype(BF16), hi.astype(BF16)


def _dispatch_copy(slot_ref, hbuf, hs_hbm, sem, r, k):
    slot = slot_ref[0, 0, k * hbuf.shape[0] + r]
    return pltpu.make_async_copy(hbuf.at[pl.ds(r, 1)], hs_hbm.at[pl.ds(slot, 1)], sem)


def _pad_copy(pad_ref, zbuf, hs_hbm, sem, e):
    return pltpu.make_async_copy(zbuf, hs_hbm.at[pl.ds(pl.multiple_of(pad_ref[e], SUBLANES), zbuf.shape[0])], sem)


def _dispatch_kernel(slot_ref, pad_ref, x_ref, mod_ref, g_ref, hs_out, hbuf, zbuf, sem, pad_sem):
    tm = x_ref.shape[1]

    @pl.when((pl.program_id(0) == 0) & (pl.program_id(1) == 0))
    def _():
        zbuf[...] = jnp.zeros(zbuf.shape, zbuf.dtype)
        for e in range(N_EXPERTS):
            _pad_copy(pad_ref, zbuf, hs_out, pad_sem, e).start()
        for e in range(N_EXPERTS):
            _pad_copy(pad_ref, zbuf, hs_out, pad_sem, e).wait()

        def clear_tile(t, c):
            tile = pltpu.make_async_copy(
                zbuf, hs_out.at[pl.ds(pl.multiple_of(t * zbuf.shape[0], zbuf.shape[0]), zbuf.shape[0])], pad_sem)
            tile.start()
            tile.wait()
            return c

        lax.fori_loop(pad_ref[N_EXPERTS], hs_out.shape[0] // zbuf.shape[0], clear_tile, 0)

    mod = mod_ref[0]
    hbuf[...] = _pack_bf16_pairs(_rms_mod(x_ref[0], g_ref[...], mod[3:4], mod[4:5]))

    def start(r, c):
        for k in range(TOP_K):
            _dispatch_copy(slot_ref, hbuf, hs_out, sem, r, k).start()
        return c

    def wait(r, c):
        for k in range(TOP_K):
            _dispatch_copy(slot_ref, hbuf, hs_out, sem, r, k).wait()
        return c

    lax.fori_loop(0, tm, start, 0, unroll=DMA_UNROLL)
    lax.fori_loop(0, tm, wait, 0, unroll=DMA_UNROLL)


def _dispatch(slots, pad_start, x, mod, g, n_slots, tm, pad_rows):
    bsz, s, d = x.shape
    nt = s // tm
    return pl.pallas_call(
        _dispatch_kernel,
        grid=(bsz, nt),
        in_specs=[pl.BlockSpec((1, 1, TOP_K * tm), lambda b, j: (b * nt + j, 0, 0), memory_space=pltpu.SMEM),
                  pl.BlockSpec(memory_space=pltpu.SMEM),
                  pl.BlockSpec((1, tm, d), lambda b, j: (b, j, 0)),
                  pl.BlockSpec((1, 6, d), lambda b, j: (b, 0, 0)),
                  pl.BlockSpec((1, d), lambda b, j: (0, 0))],
        out_specs=pl.BlockSpec(memory_space=pl.ANY),
        out_shape=jax.ShapeDtypeStruct((n_slots, d // 2), jnp.uint32),
        scratch_shapes=[pltpu.VMEM((tm, d // 2), jnp.uint32), pltpu.VMEM((pad_rows, d // 2), jnp.uint32),
                        pltpu.SemaphoreType.DMA(()), pltpu.SemaphoreType.DMA(())],
        compiler_params=_cparams(("arbitrary", "arbitrary")),
        name="moe_dispatch",
    )(slots, pad_start, x, mod, g.reshape(1, d))


def _moe_kernel(te_ref, nv_ref, hs_ref, wa_ref, wg_ref, w2_ref, o_ref, h_scr, acc_scr):
    del te_ref
    i = pl.program_id(0)
    j = pl.program_id(1)
    valid = i < nv_ref[0]

    @pl.when(valid & (j == 0))
    def _():
        half = hs_ref.shape[1]
        lo, hi = _unpack_bf16_pairs(hs_ref[...])
        h_scr[:, :half] = lo
        h_scr[:, half:] = hi
        acc_scr[...] = jnp.zeros_like(acc_scr)

    @pl.when(valid)
    def _():
        h = h_scr[...]
        a = jnp.dot(h, wa_ref[0].astype(BF16), preferred_element_type=F32)
        g = jnp.dot(h, wg_ref[0].astype(BF16), preferred_element_type=F32)
        acc_scr[...] += jnp.dot((_silu(a) * g).astype(BF16), w2_ref[0].astype(BF16), preferred_element_type=F32)

    last = j == pl.num_programs(1) - 1

    @pl.when(valid & last)
    def _():
        o_ref[...] = acc_scr[...]

    @pl.when(jnp.logical_not(valid) & last)
    def _():
        o_ref[...] = jnp.zeros_like(o_ref)


def _moe_ffn(tile_expert, n_valid, hs, w13, w2, tm, tf):
    n_slots, half = hs.shape
    d = 2 * half
    f = w2.shape[1]
    nf = f // tf
    nt = n_slots // tm

    def row_idx(i, j, te, nv):
        return (jnp.minimum(i, nv[0] - 1), 0)

    def ff(i, j, nv):
        return jnp.where(i < nv[0], j, nf - 1)

    grid_spec = pltpu.PrefetchScalarGridSpec(
        num_scalar_prefetch=2,
        grid=(nt, nf),
        in_specs=[pl.BlockSpec((tm, half), row_idx),
                  pl.BlockSpec((1, d, tf), lambda i, j, te, nv: (te[i], 0, ff(i, j, nv))),
                  pl.BlockSpec((1, d, tf), lambda i, j, te, nv: (te[i], 0, nf + ff(i, j, nv))),
                  pl.BlockSpec((1, tf, d), lambda i, j, te, nv: (te[i], ff(i, j, nv), 0))],
        out_specs=pl.BlockSpec((tm, d), lambda i, j, te, nv: (i, 0)),
        scratch_shapes=[pltpu.VMEM((tm, d), BF16), pltpu.VMEM((tm, d), F32)],
    )
    return pl.pallas_call(
        _moe_kernel,
        grid_spec=grid_spec,
        out_shape=jax.ShapeDtypeStruct((n_slots, d), F32),
        compiler_params=_cparams(("arbitrary", "arbitrary")),
        name="moe_ffn",
    )(tile_expert, n_valid, hs, w13, w13, w2)


def _combine_copy(slot_ref, ys_hbm, buf, sems, par, r, k):
    slot = slot_ref[0, 0, k * buf.shape[2] + r]
    return pltpu.make_async_copy(ys_hbm.at[pl.ds(slot, 1)], buf.at[par, k, pl.ds(r, 1)], sems.at[par])


def _combine_kernel(slot_ref, next_slot_ref, x_ref, route_ref, mod_ref, g_ref, ys_hbm, o_ref, buf, sems):
    tc = x_ref.shape[1]
    g = pl.program_id(0) * pl.num_programs(1) + pl.program_id(1)
    n_steps = pl.num_programs(0) * pl.num_programs(1)
    par = lax.rem(g, 2)

    def fetch(slots, parity):
        def start(r, c):
            for k in range(TOP_K):
                _combine_copy(slots, ys_hbm, buf, sems, parity, r, k).start()
            return c
        lax.fori_loop(0, tc, start, 0, unroll=DMA_UNROLL)

    @pl.when(g == 0)
    def _():
        fetch(slot_ref, par)

    @pl.when(g + 1 < n_steps)
    def _():
        fetch(next_slot_ref, 1 - par)

    def wait(r, c):
        for k in range(TOP_K):
            _combine_copy(slot_ref, ys_hbm, buf, sems, par, r, k).wait()
        return c

    lax.fori_loop(0, tc, wait, 0, unroll=DMA_UNROLL)

    route = route_ref[0]
    y = route[:, 2:3] * buf[par, 0] + route[:, 3:4] * buf[par, 1]
    x = x_ref[0] + mod_ref[0][5:6] * y
    o_ref[0] = (x * lax.rsqrt(jnp.mean(x * x, axis=-1, keepdims=True) + NORM_EPS)) * g_ref[...]


def _combine(slots, x, route, mod, g, ys, tc):
    bsz, s, d = x.shape
    nt = s // tc
    tile = lambda b, j: (b, j, 0)
    return pl.pallas_call(
        _combine_kernel,
        grid=(bsz, nt),
        in_specs=[pl.BlockSpec((1, 1, TOP_K * tc), lambda b, j: (b * nt + j, 0, 0), memory_space=pltpu.SMEM),
                  pl.BlockSpec((1, 1, TOP_K * tc), lambda b, j: (jnp.minimum(b * nt + j + 1, bsz * nt - 1), 0, 0),
                               memory_space=pltpu.SMEM),
                  pl.BlockSpec((1, tc, d), tile),
                  pl.BlockSpec((1, tc, LANES), tile),
                  pl.BlockSpec((1, 6, d), lambda b, j: (b, 0, 0)),
                  pl.BlockSpec((1, d), lambda b, j: (0, 0)),
                  pl.BlockSpec(memory_space=pl.ANY)],
        out_specs=pl.BlockSpec((1, tc, d), tile),
        out_shape=jax.ShapeDtypeStruct((bsz, s, d), F32),
        scratch_shapes=[pltpu.VMEM((2, TOP_K, tc, d), F32), pltpu.SemaphoreType.DMA((2,))],
        compiler_params=_cparams(("arbitrary", "arbitrary")),
        name="moe_combine",
    )(slots, slots, x, route, mod, g.reshape(1, d), ys)


def _routing_plan(ids, tm):
    n_tok_tiles, _, tok_tile = ids.shape
    t = n_tok_tiles * tok_tile
    flat_e = ids[:, :TOP_K, :].astype(jnp.int32).transpose(1, 0, 2).reshape(-1)
    onehot = (flat_e[:, None] == jnp.arange(N_EXPERTS, dtype=jnp.int32)[None, :]).astype(jnp.int32)
    csum = jnp.cumsum(onehot, axis=0)
    rank = jnp.sum((csum - onehot) * onehot, axis=1)
    counts = csum[-1]
    padded = ((counts + tm - 1) // tm) * tm
    ends = jnp.cumsum(padded)
    starts = ends - padded
    slots = jnp.sum(starts[None, :] * onehot, axis=1) + rank
    n_tiles = TOP_K * t // tm + N_EXPERTS
    tile_start = jnp.arange(n_tiles, dtype=jnp.int32) * tm
    n_valid = (ends[-1] // tm).astype(jnp.int32)
    tile_expert = jnp.sum((tile_start[:, None] >= ends[None, :]).astype(jnp.int32), axis=1)
    last_expert = jnp.sum((ends[-1] - 1 >= ends).astype(jnp.int32))
    tile_expert = jnp.where(tile_start < ends[-1], tile_expert, last_expert).astype(jnp.int32)
    pad_start = ((starts + counts) // SUBLANES * SUBLANES).astype(jnp.int32)
    pad_start = jnp.concatenate([pad_start, n_valid.reshape(1)])
    slots = slots.astype(jnp.int32).reshape(TOP_K, n_tok_tiles, tok_tile).transpose(1, 0, 2)
    slots = slots.reshape(n_tok_tiles, 1, TOP_K * tok_tile)
    return slots, tile_expert, n_valid.reshape(1), n_tiles, pad_start


def kernel(x, c, positions, ada_w, ada_b, norm1_g, norm2_g, conv_w1, conv_b1, conv_dw_w, conv_dw_b,
           conv_ln_g, conv_ln_b, conv_w2, conv_b2, kv_ada_w, kv_ada_b, kv_norm_g, w_kv, w_q, w_o,
           ffn_w13, ffn_w2, router_w, router_b, moe_w13, moe_w2, final_g):
    bsz, s, d = x.shape
    assert ada_w.shape[0] == 2 and d == N_HEADS * HEAD_DIM and s % MOBA_BLOCK == 0
    nb = s // MOBA_BLOCK
    topk = min(MOBA_TOPK, max(nb - 1, 1))
    tm, moe_tm = ROW_TILE, MOE_ROW_TILE
    assert s % tm == 0 and tm % MOBA_BLOCK == 0 and (TOP_K * bsz * s) % moe_tm == 0

    mod0 = _ada(c, ada_w, 0, ada_b[0]).reshape(bsz, 6, d)
    mod1 = _ada(c, ada_w, 1, ada_b[1]).reshape(bsz, 6, d)
    modkv = _ada(c, kv_ada_w[None], 0, kv_ada_b).reshape(bsz, 2, d)

    x = _conv_mixer(x, mod0, norm1_g[0], conv_w1[0].astype(BF16), conv_b1[0], conv_dw_w[0], conv_dw_b[0],
                    conv_ln_g[0], conv_ln_b[0], conv_w2[0].astype(BF16), conv_b2[0], tm)
    x = _dense_ffn(x, mod0, norm2_g[0], ffn_w13[0].astype(BF16), ffn_w2[0].astype(BF16), tm, FFN_CHUNK)

    inv_freq = ROPE_THETA ** (-jnp.arange(0, ROT_DIM, 2, dtype=F32) / ROT_DIM)
    freq = jnp.tile(inv_freq, HEAD_DIM // (ROT_DIM // 2)).reshape(1, HEAD_DIM)
    q, k, vt, kmean = _kvq(x, modkv, mod1, kv_norm_g, norm1_g[1], w_kv.astype(BF16), w_q[0].astype(BF16),
                          positions.reshape(bsz, s, 1), freq, tm)
    kmean = kmean.reshape(bsz, nb, N_HEADS, HEAD_DIM).transpose(0, 2, 1, 3).astype(BF16)
    kmean = jnp.pad(kmean, ((0, 0), (0, 0), (0, -nb % BF16_ROWS), (0, 0)))

    o = _attention(q, k, vt, kmean, topk)
    rw = jnp.zeros((d, LANES), BF16).at[:, :N_EXPERTS].set(router_w[0].astype(BF16))
    rb = jnp.zeros((1, LANES), F32).at[0, :N_EXPERTS].set(router_b[0])
    x, route, ids = _oproj_router(o, x, mod1, norm2_g[1], w_o[0].astype(BF16), rw, rb, tm)

    slots, tile_expert, n_valid, n_tiles, pad_start = _routing_plan(ids, moe_tm)
    hs = _dispatch(slots, pad_start, x, mod1, norm2_g[1], n_tiles * moe_tm, tm, moe_tm)
    ys = _moe_ffn(tile_expert, n_valid, hs, moe_w13[0], moe_w2[0], moe_tm, MOE_FF_TILE)
    return _combine(slots, x, route, mod1, final_g, ys, tm)
```
